```python
import math
import jax, jax.numpy as jnp
from jax import lax
import numpy as np

D_MODEL = 2048
BATCH = 4
SEQ = 4096
DEPTH = 4

CHUNK = 64
Q_BLOCK = 128
N_MIXERS = 3
N_A = (DEPTH + 2) // 3
N_B = (DEPTH + 1) // 3
N_C = DEPTH // 3
PLE_DIM = 256
ROPE_THETA = 10000.0
EPS = 1e-6

DA_HEAD_DIM = 128
DA_V_DIM = 2 * DA_HEAD_DIM
DA_HEADS = D_MODEL // DA_V_DIM
DA_WIDTH = DA_HEADS * DA_V_DIM
MLA_NOPE = 128
MLA_ROPE = 64
MLA_V = 128
MLA_HEADS = D_MODEL // MLA_V
MLA_Q_RANK = 512
MLA_KV_RANK = 512
MLA_WIDTH = MLA_HEADS * MLA_V
LRU_WIDTH = D_MODEL
LRU_BLOCKS = 8
LRU_BLOCK = LRU_WIDTH // LRU_BLOCKS
CONV_WIDTH = 4
LRU_C = 8.0

kernel_name = "hybrid_diffattn_mla_rglru_streaming"

F32 = jnp.float32


def rms_norm(x, g):
    xf = x.astype(F32)
    y = xf * lax.rsqrt(jnp.mean(xf * xf, axis=-1, keepdims=True) + EPS)
    return (y * g.astype(F32)).astype(x.dtype)


def rope(x, positions):
    d = x.shape[-1]
    half = d // 2
    inv = ROPE_THETA ** (-jnp.arange(half, dtype=F32) / half)
    ang = positions.astype(F32)[..., None] * inv
    ang = ang.reshape(ang.shape[:2] + (1,) * (x.ndim - 3) + (half,))
    cos, sin = jnp.cos(ang), jnp.sin(ang)
    xf = x.astype(F32)
    x1, x2 = xf[..., :half], xf[..., half:]
    return jnp.concatenate([x1 * cos - x2 * sin, x2 * cos + x1 * sin], axis=-1).astype(x.dtype)


def chunk_mask(q_start, q_len, k_len):
    qc = (q_start + jnp.arange(q_len)) // CHUNK
    kc = jnp.arange(k_len) // CHUNK
    return kc[None, :] <= qc[:, None]


def block_sweep(attend, seq):
    return jnp.concatenate([attend(s, s + Q_BLOCK) for s in range(0, seq, Q_BLOCK)], axis=1)


def diff_attention(xn, positions, w_in, q_gain, k_gain, lq1, lk1, lq2, lk2, sub_gain, w_out, layer_idx):
    B, S, _ = xn.shape
    H, d = DA_HEADS, DA_HEAD_DIM
    u = xn @ w_in
    q, k, v, gate = jnp.split(u, 4, axis=-1)
    q = rope(rms_norm(q.reshape(B, S, H, 2, d), q_gain), positions)
    k = rope(rms_norm(k.reshape(B, S, H, 2, d), k_gain), positions)
    v = v.reshape(B, S, H, DA_V_DIM)
    lam_init = 0.8 - 0.6 * math.exp(-0.3 * layer_idx)
    lam = (jnp.exp(jnp.sum(lq1.astype(F32) * lk1.astype(F32)))
           - jnp.exp(jnp.sum(lq2.astype(F32) * lk2.astype(F32))) + lam_init)
    scale = d ** -0.5

    def attend(s, e):
        sc = jnp.einsum('bqhcd,bkhcd->bhcqk', q[:, s:e].astype(F32), k[:, :e].astype(F32)) * scale
        sc = jnp.where(chunk_mask(s, e - s, e), sc, -jnp.inf)
        pr = jax.nn.softmax(sc, axis=-1)
        w = pr[:, :, 0] - lam * pr[:, :, 1]
        return jnp.einsum('bhqk,bkhe->bqhe', w, v[:, :e].astype(F32))

    o = block_sweep(attend, S)
    o = rms_norm(o, sub_gain) * (1.0 - lam_init)
    y = o.reshape(B, S, DA_WIDTH).astype(xn.dtype) * jax.nn.silu(gate)
    return y @ w_out


def mla(xn, positions, w_in, cq_gain, ckv_gain, w_uq, w_ukv, qn_gain, qr_gain, kn_gain, kr_gain, w_out):
    B, S, _ = xn.shape
    H = MLA_HEADS
    u = xn @ w_in
    c_q, c_kv, k_pe, gate = jnp.split(
        u, [MLA_Q_RANK, MLA_Q_RANK + MLA_KV_RANK, MLA_Q_RANK + MLA_KV_RANK + MLA_ROPE], axis=-1)
    q = (rms_norm(c_q, cq_gain) @ w_uq).reshape(B, S, H, MLA_NOPE + MLA_ROPE)
    kv = (rms_norm(c_kv, ckv_gain) @ w_ukv).reshape(B, S, H, MLA_NOPE + MLA_V)
    q_nope = rms_norm(q[..., :MLA_NOPE], qn_gain)
    q_pe = rope(rms_norm(q[..., MLA_NOPE:], qr_gain), positions)
    k_nope = rms_norm(kv[..., :MLA_NOPE], kn_gain)
    v = kv[..., MLA_NOPE:]
    k_pe = rope(rms_norm(k_pe, kr_gain), positions)
    scale = (MLA_NOPE + MLA_ROPE) ** -0.5

    def attend(s, e):
        sc = (jnp.einsum('bqhd,bkhd->bhqk', q_nope[:, s:e].astype(F32), k_nope[:, :e].astype(F32))
              + jnp.einsum('bqhr,bkr->bhqk', q_pe[:, s:e].astype(F32), k_pe[:, :e].astype(F32))) * scale
        sc = jnp.where(chunk_mask(s, e - s, e), sc, -jnp.inf)
        pr = jax.nn.softmax(sc, axis=-1)
        return jnp.einsum('bhqk,bkhe->bqhe', pr, v[:, :e].astype(F32))

    o = block_sweep(attend, S)
    y = o.reshape(B, S, MLA_WIDTH).astype(xn.dtype) * jax.nn.silu(gate)
    return y @ w_out


def rglru_block(xn, w_in, conv_w, conv_b, w_a, b_a, w_x, b_x, lam, w_out):
    B, S, _ = xn.shape
    u = xn @ w_in
    xb, gate = jnp.split(u, 2, axis=-1)
    xc = lax.conv_general_dilated(
        xb.astype(F32), conv_w.astype(F32)[:, None, :], window_strides=(1,),
        padding=[(CONV_WIDTH - 1, 0)], dimension_numbers=('NWC', 'WIO', 'NWC'),
        feature_group_count=LRU_WIDTH) + conv_b.astype(F32)
    xg = xc.reshape(B, S, LRU_BLOCKS, LRU_BLOCK)
    r = jax.nn.sigmoid(jnp.einsum('bsgi,gij->bsgj', xg, w_a.astype(F32)).reshape(B, S, LRU_WIDTH)
                       + b_a.astype(F32))
    i_gate = jax.nn.sigmoid(jnp.einsum('bsgi,gij->bsgj', xg, w_x.astype(F32)).reshape(B, S, LRU_WIDTH)
                            + b_x.astype(F32))
    log_a = -LRU_C * r * jax.nn.softplus(-lam.astype(F32))
    a = jnp.exp(log_a)
    bt = jnp.sqrt(-jnp.expm1(2.0 * log_a)) * (i_gate * xc)

    def combine(left, right):
        a1, b1 = left
        a2, b2 = right
        return a1 * a2, a2 * b1 + b2

    _, h = lax.associative_scan(combine, (a, bt), axis=1)
    y = h.astype(xn.dtype) * jax.nn.silu(gate)
    return y @ w_out


def setup_inputs(seed: int = 0) -> dict:
    key = jax.random.key(seed)
    ks = iter(jax.random.split(key, 48))

    def nrm(shape, scale):
        return jax.random.normal(next(ks), shape, F32) * scale

    def gain(shape):
        return 1.0 + 0.05 * jax.random.normal(next(ks), shape, F32)

    D = D_MODEL
    x = nrm((BATCH, SEQ, D), 1.0)
    p = nrm((DEPTH, BATCH, SEQ, PLE_DIM), 1.0)
    offset = jax.random.randint(next(ks), (BATCH, 1), 0, 4096, dtype=jnp.int32)
    positions = (offset + jnp.arange(SEQ, dtype=jnp.int32)[None, :]).astype(jnp.int32)
    norm_gain = gain((DEPTH, D))
    a_w_in = nrm((N_A, D, 4 * DA_WIDTH), D ** -0.5)
    a_q_norm = gain((N_A, DA_HEAD_DIM))
    a_k_norm = gain((N_A, DA_HEAD_DIM))
    a_lambda_q1 = nrm((N_A, DA_HEAD_DIM), 0.1)
    a_lambda_k1 = nrm((N_A, DA_HEAD_DIM), 0.1)
    a_lambda_q2 = nrm((N_A, DA_HEAD_DIM), 0.1)
    a_lambda_k2 = nrm((N_A, DA_HEAD_DIM), 0.1)
    a_sub_norm = gain((N_A, DA_V_DIM))
    a_w_out = nrm((N_A, DA_WIDTH, D), DA_WIDTH ** -0.5)
    b_w_in = nrm((N_B, D, MLA_Q_RANK + MLA_KV_RANK + MLA_ROPE + MLA_WIDTH), D ** -0.5)
    b_cq_norm = gain((N_B, MLA_Q_RANK))
    b_ckv_norm = gain((N_B, MLA_KV_RANK))
    b_w_uq = nrm((N_B, MLA_Q_RANK, MLA_HEADS * (MLA_NOPE + MLA_ROPE)), MLA_Q_RANK ** -0.5)
    b_w_ukv = nrm((N_B, MLA_KV_RANK, MLA_HEADS * (MLA_NOPE + MLA_V)), MLA_KV_RANK ** -0.5)
    b_q_nope_norm = gain((N_B, MLA_NOPE))
    b_q_rope_norm = gain((N_B, MLA_ROPE))
    b_k_nope_norm = gain((N_B, MLA_NOPE))
    b_k_rope_norm = gain((N_B, MLA_ROPE))
    b_w_out = nrm((N_B, MLA_WIDTH, D), MLA_WIDTH ** -0.5)
    c_w_in = nrm((N_C, D, 2 * LRU_WIDTH), D ** -0.5)
    c_conv_w = nrm((N_C, CONV_WIDTH, LRU_WIDTH), CONV_WIDTH ** -0.5)
    c_conv_b = nrm((N_C, LRU_WIDTH), 0.02)
    c_w_a = nrm((N_C, LRU_BLOCKS, LRU_BLOCK, LRU_BLOCK), LRU_BLOCK ** -0.5)
    c_b_a = nrm((N_C, LRU_WIDTH), 0.02)
    c_w_x = nrm((N_C, LRU_BLOCKS, LRU_BLOCK, LRU_BLOCK), LRU_BLOCK ** -0.5)
    c_b_x = nrm((N_C, LRU_WIDTH), 0.02)
    a_target = jax.random.uniform(next(ks), (N_C, LRU_WIDTH), F32, minval=0.9, maxval=0.999)
    s = a_target ** (1.0 / LRU_C)
    c_lambda = jnp.log(s) - jnp.log1p(-s)
    c_w_out = nrm((N_C, LRU_WIDTH, D), LRU_WIDTH ** -0.5)
    ple_norm = gain((DEPTH, D))
    ple_w_gate = nrm((DEPTH, D, D), D ** -0.5)
    ple_w_proj = nrm((DEPTH, PLE_DIM, D), PLE_DIM ** -0.5)
    return {
        "x": x, "p": p, "positions": positions, "norm_gain": norm_gain,
        "a_w_in": a_w_in, "a_q_norm": a_q_norm, "a_k_norm": a_k_norm,
        "a_lambda_q1": a_lambda_q1, "a_lambda_k1": a_lambda_k1,
        "a_lambda_q2": a_lambda_q2, "a_lambda_k2": a_lambda_k2,
        "a_sub_norm": a_sub_norm, "a_w_out": a_w_out,
        "b_w_in": b_w_in, "b_cq_norm": b_cq_norm, "b_ckv_norm": b_ckv_norm,
        "b_w_uq": b_w_uq, "b_w_ukv": b_w_ukv, "b_q_nope_norm": b_q_nope_norm,
        "b_q_rope_norm": b_q_rope_norm, "b_k_nope_norm": b_k_nope_norm,
        "b_k_rope_norm": b_k_rope_norm, "b_w_out": b_w_out,
        "c_w_in": c_w_in, "c_conv_w": c_conv_w, "c_conv_b": c_conv_b,
        "c_w_a": c_w_a, "c_b_a": c_b_a, "c_w_x": c_w_x, "c_b_x": c_b_x,
        "c_lambda": c_lambda, "c_w_out": c_w_out,
        "ple_norm": ple_norm, "ple_w_gate": ple_w_gate, "ple_w_proj": ple_w_proj,
    }


def reference(x, p, positions, norm_gain,
              a_w_in, a_q_norm, a_k_norm, a_lambda_q1, a_lambda_k1, a_lambda_q2, a_lambda_k2,
              a_sub_norm, a_w_out,
              b_w_in, b_cq_norm, b_ckv_norm, b_w_uq, b_w_ukv, b_q_nope_norm, b_q_rope_norm,
              b_k_nope_norm, b_k_rope_norm, b_w_out,
              c_w_in, c_conv_w, c_conv_b, c_w_a, c_b_a, c_w_x, c_b_x, c_lambda, c_w_out,
              ple_norm, ple_w_gate, ple_w_proj):
    h = x
    for i in range(DEPTH):
        xn = rms_norm(h, norm_gain[i])
        j = i // N_MIXERS
        kind = i % N_MIXERS
        if kind == 0:
            m = diff_attention(xn, positions, a_w_in[j], a_q_norm[j], a_k_norm[j],
                               a_lambda_q1[j], a_lambda_k1[j], a_lambda_q2[j], a_lambda_k2[j],
                               a_sub_norm[j], a_w_out[j], i)
        elif kind == 1:
            m = mla(xn, positions, b_w_in[j], b_cq_norm[j], b_ckv_norm[j], b_w_uq[j], b_w_ukv[j],
                    b_q_nope_norm[j], b_q_rope_norm[j], b_k_nope_norm[j], b_k_rope_norm[j],
                    b_w_out[j])
        else:
            m = rglru_block(xn, c_w_in[j], c_conv_w[j], c_conv_b[j], c_w_a[j], c_b_a[j],
                            c_w_x[j], c_b_x[j], c_lambda[j], c_w_out[j])
        h = h + m.astype(h.dtype)
        g = jax.nn.sigmoid((rms_norm(h, ple_norm[i]) @ ple_w_gate[i]).astype(F32))
        h = h + (g * (p[i] @ ple_w_proj[i]).astype(F32)).astype(h.dtype)
    return h
```

```python
import functools
import math

import jax
import jax.numpy as jnp
from jax import lax
from jax.experimental import pallas as pl
from jax.experimental.pallas import tpu as pltpu

F32 = jnp.float32
BF16 = jnp.bfloat16

D_MODEL = 2048
CHUNK = 64
ROPE_THETA = 10000.0
EPS = 1e-6
N_MIXERS = 3

DA_HEAD_DIM = 128
DA_V_DIM = 256
DA_HEADS = 8
MLA_NOPE = 128
MLA_ROPE = 64
MLA_V = 128
MLA_HEADS = 16
MLA_Q_RANK = 512
MLA_KV_RANK = 512
LRU_BLOCKS = 8
LRU_BLOCK = 256
CONV_WIDTH = 4
LRU_C = 8.0

LANES = 128
VMEM_LIMIT = 56 * 1024 * 1024
ROW_TILE = 1024
COL_TILE = 512
ATTN_BLOCK = 256
LRU_TIME_TILE = 256


def _cparams(*sem):
    return pltpu.CompilerParams(dimension_semantics=sem, vmem_limit_bytes=VMEM_LIMIT)


def _rope_table_kernel(pos_ref, inv_da_ref, inv_mla_ref, c_da, s_da, c_m, s_m1, s_m2):
    pos = pos_ref[...].astype(F32)
    lane = lax.broadcasted_iota(jnp.int32, c_da.shape, 1)
    ang = pos * inv_da_ref[...]
    sn = jnp.sin(ang)
    c_da[...] = jnp.cos(ang)
    s_da[...] = jnp.where(lane < 64, -sn, sn)
    ang = pos * inv_mla_ref[...]
    cs = jnp.cos(ang)
    sn = jnp.sin(ang)
    c_m[...] = jnp.where(lane < 64, cs, 0.0)
    s_m1[...] = jnp.where(lane < 32, -sn, 0.0)
    s_m2[...] = jnp.where((lane >= 32) & (lane < 64), sn, 0.0)


def _rope_tables(pos_col):
    T = pos_col.shape[0]
    tr = min(2048, T)
    half = DA_HEAD_DIM // 2
    inv = ROPE_THETA ** (-jnp.arange(half, dtype=F32) / half)
    inv_da = jnp.concatenate([inv, inv])[None, :]
    half = MLA_ROPE // 2
    inv = ROPE_THETA ** (-jnp.arange(half, dtype=F32) / half)
    inv_mla = jnp.concatenate([inv, inv, inv, inv])[None, :]
    tab = jax.ShapeDtypeStruct((T, LANES), F32)
    row = pl.BlockSpec((tr, LANES), lambda i: (i, 0))
    vec = pl.BlockSpec((1, LANES), lambda i: (0, 0))
    return pl.pallas_call(
        _rope_table_kernel,
        out_shape=[tab] * 5,
        grid=(T // tr,),
        in_specs=[pl.BlockSpec((tr, 1), lambda i: (i, 0)), vec, vec],
        out_specs=[row] * 5,
        compiler_params=_cparams("parallel"),
        name="rope_tables",
    )(pos_col, inv_da, inv_mla)


def _proj_kernel(x_ref, g_ref, w_ref, *rest, n_extra, n_out, norm, epilogue):
    extra = rest[:n_extra]
    outs = rest[n_extra:n_extra + n_out]
    j = pl.program_id(1)
    if norm:
        xn_ref = rest[-1]

        @pl.when(j == 0)
        def _():
            x = x_ref[...].astype(F32)
            ms = jnp.mean(x * x, axis=-1, keepdims=True)
            xn_ref[...] = (x * lax.rsqrt(ms + EPS) * g_ref[...]).astype(BF16)

        lhs = xn_ref[...]
    else:
        lhs = x_ref[...]
    acc = jnp.dot(lhs, w_ref[...], preferred_element_type=F32)
    epilogue(acc, j, extra, outs)


def _projection(x, w, *, epilogue, out_shapes, out_specs, gain=None, x_col_block=0,
                extras=(), tm=ROW_TILE, tn=COL_TILE, name):
    T = x.shape[0]
    K, N = w.shape
    tm = min(tm, T)
    tn = min(tn, N)
    norm = gain is not None
    if not norm:
        gain = jnp.zeros((1, K), F32)
    in_specs = [
        pl.BlockSpec((tm, K), lambda i, j: (i, x_col_block)),
        pl.BlockSpec((1, K), lambda i, j: (0, 0)),
        pl.BlockSpec((K, tn), lambda i, j: (0, j)),
    ] + [spec for _, spec in extras]
    kern = functools.partial(_proj_kernel, n_extra=len(extras), n_out=len(out_shapes),
                             norm=norm, epilogue=epilogue)
    return pl.pallas_call(
        kern,
        out_shape=out_shapes,
        grid=(T // tm, N // tn),
        in_specs=in_specs,
        out_specs=out_specs,
        scratch_shapes=[pltpu.VMEM((tm, K), BF16)] if norm else [],
        compiler_params=_cparams("parallel", "arbitrary"),
        name=name,
    )(x, gain.reshape(1, K).astype(F32), w, *[a for a, _ in extras])


def _rms(x, width):
    return x * lax.rsqrt(jnp.sum(x * x, axis=-1, keepdims=True) * (1.0 / width) + EPS)


def _silu(x):
    return x * jax.nn.sigmoid(x)


def _da_in_epilogue(acc, j, extra, outs):
    g_ref, c_ref, s_ref = extra
    (o_ref,) = outs
    n_qk = 2 * DA_HEADS * 2 * DA_HEAD_DIM // COL_TILE
    n_v = DA_HEADS * DA_V_DIM // COL_TILE

    @pl.when(j < n_qk)
    def _():
        g = g_ref[0]
        c = c_ref[...]
        s = s_ref[...]
        for t in range(COL_TILE // DA_HEAD_DIM):
            sl = slice(t * DA_HEAD_DIM, (t + 1) * DA_HEAD_DIM)
            xn = _rms(acc[:, sl], DA_HEAD_DIM) * g
            o_ref[:, sl] = (xn * c + pltpu.roll(xn, 64, 1) * s).astype(BF16)

    @pl.when((j >= n_qk) & (j < n_qk + n_v))
    def _():
        o_ref[...] = acc.astype(BF16)

    @pl.when(j >= n_qk + n_v)
    def _():
        o_ref[...] = _silu(acc).astype(BF16)


def _silu_epilogue(acc, j, extra, outs):
    outs[0][...] = _silu(acc).astype(BF16)


def _f32_epilogue(acc, j, extra, outs):
    outs[0][...] = acc


def _mla_rope(zn, c_m, s_m1, s_m2):
    return zn * c_m + pltpu.roll(zn, 96, 1) * s_m1 + pltpu.roll(zn, 32, 1) * s_m2


def _mla_q_epilogue(acc, j, extra, outs):
    gn_ref, gr_ref, c_ref, s1_ref, s2_ref = extra
    (o_ref,) = outs
    o_ref[:, :MLA_NOPE] = (_rms(acc[:, :MLA_NOPE], MLA_NOPE) * gn_ref[...]).astype(BF16)
    zn = _rms(acc[:, MLA_NOPE:], MLA_ROPE) * gr_ref[...]
    o_ref[:, MLA_NOPE:] = _mla_rope(zn, c_ref[...], s1_ref[...], s2_ref[...]).astype(BF16)


def _mla_kv_epilogue(acc, j, extra, outs):
    gn_ref, gr_ref, kpe_ref, c_ref, s1_ref, s2_ref = extra
    k_ref, v_ref = outs
    k_ref[:, :MLA_NOPE] = (_rms(acc[:, :MLA_NOPE], MLA_NOPE) * gn_ref[...]).astype(BF16)
    zn = _rms(kpe_ref[...], MLA_ROPE) * gr_ref[...]
    k_ref[:, MLA_NOPE:] = _mla_rope(zn, c_ref[...], s1_ref[...], s2_ref[...]).astype(BF16)
    v_ref[...] = acc[:, MLA_NOPE:].astype(BF16)


def _residual_epilogue(acc, j, extra, outs):
    outs[0][...] = extra[0][...] + acc


def _ple_epilogue(acc, j, extra, outs):
    h_ref, p_ref, wp_ref = extra
    proj = jnp.dot(p_ref[...].astype(BF16), wp_ref[...], preferred_element_type=F32)
    outs[0][...] = h_ref[...] + jax.nn.sigmoid(acc) * proj


def _chunk_mask(bq):
    r = lax.broadcasted_iota(jnp.int32, (bq, bq), 0) // CHUNK
    c = lax.broadcasted_iota(jnp.int32, (bq, bq), 1) // CHUNK
    return c <= r


def _flash_step(q, k, v, mask, m_ref, l_ref, acc_ref, c):
    s = lax.dot_general(q, k, (((1,), (1,)), ((), ())), preferred_element_type=F32)
    if mask is not None:
        s = jnp.where(mask, s, -jnp.inf)
    m_prev = m_ref[c]
    m_new = jnp.maximum(m_prev, jnp.max(s, axis=-1, keepdims=True))
    alpha = jnp.exp(m_prev - m_new)
    p = jnp.exp(s - m_new)
    l_ref[c] = alpha * l_ref[c] + jnp.sum(p, axis=-1, keepdims=True)
    acc_ref[c] = alpha * acc_ref[c] + jnp.dot(p.astype(BF16), v, preferred_element_type=F32)
    m_ref[c] = m_new


def _flash_init(m_ref, l_ref, acc_ref):
    m_ref[...] = jnp.full(m_ref.shape, -jnp.inf, F32)
    l_ref[...] = jnp.zeros(l_ref.shape, F32)
    acc_ref[...] = jnp.zeros(acc_ref.shape, F32)


def _da_attn_kernel(q_ref, k_ref, v_ref, gate_ref, lq1_ref, lk1_ref, lq2_ref, lk2_ref, sg_ref,
                    o_ref, m_ref, l_ref, acc_ref, *, bq, lam_init):
    i = pl.program_id(2)
    d = DA_HEAD_DIM
    _flash_init(m_ref, l_ref, acc_ref)
    q = q_ref[0]

    def step(kstart, mask):
        v = v_ref[0, pl.ds(kstart, bq), :]
        for c in range(2):
            k = k_ref[0, pl.ds(kstart, bq), c * d:(c + 1) * d]
            _flash_step(q[:, c * d:(c + 1) * d], k, v, mask, m_ref, l_ref, acc_ref, c)

    def body(jb, carry):
        step(pl.multiple_of(jb * bq, bq), None)
        return carry

    lax.fori_loop(0, i, body, 0)
    step(pl.multiple_of(i * bq, bq), _chunk_mask(bq))

    lam = (jnp.exp(jnp.sum(lq1_ref[...] * lk1_ref[...], axis=-1, keepdims=True))
           - jnp.exp(jnp.sum(lq2_ref[...] * lk2_ref[...], axis=-1, keepdims=True)) + lam_init)
    o = acc_ref[0] / l_ref[0] - lam * (acc_ref[1] / l_ref[1])
    o = _rms(o, DA_V_DIM) * sg_ref[...] * (1.0 - lam_init)
    o_ref[0] = (o * gate_ref[0].astype(F32)).astype(BF16)


def _da_attention(u3, lq1, lk1, lq2, lk2, sub_gain, lam_init):
    B, S, _ = u3.shape
    bq = min(ATTN_BLOCK, S)
    H = DA_HEADS
    vec = lambda n: pl.BlockSpec((1, n), lambda b, h, i: (0, 0))
    kern = functools.partial(_da_attn_kernel, bq=bq, lam_init=lam_init)
    return pl.pallas_call(
        kern,
        out_shape=jax.ShapeDtypeStruct((B, S, H * DA_V_DIM), BF16),
        grid=(B, H, S // bq),
        in_specs=[
            pl.BlockSpec((1, bq, 256), lambda b, h, i: (b, i, h)),
            pl.BlockSpec((1, S, 256), lambda b, h, i: (b, 0, H + h)),
            pl.BlockSpec((1, S, 256), lambda b, h, i: (b, 0, 2 * H + h)),
            pl.BlockSpec((1, bq, 256), lambda b, h, i: (b, i, 3 * H + h)),
            vec(128), vec(128), vec(128), vec(128), vec(256),
        ],
        out_specs=pl.BlockSpec((1, bq, 256), lambda b, h, i: (b, i, h)),
        scratch_shapes=[pltpu.VMEM((2, bq, 1), F32), pltpu.VMEM((2, bq, 1), F32),
                        pltpu.VMEM((2, bq, DA_V_DIM), F32)],
        compiler_params=_cparams("parallel", "parallel", "arbitrary"),
        name="da_attention",
    )(u3, u3, u3, u3, lq1.reshape(1, 128), lk1.reshape(1, 128), lq2.reshape(1, 128),
      lk2.reshape(1, 128), sub_gain.reshape(1, 256))


def _mla_attn_kernel(q_ref, k_ref, v_ref, gate_ref, o_ref, m_ref, l_ref, acc_ref, *, bq):
    i = pl.program_id(2)
    _flash_init(m_ref, l_ref, acc_ref)
    q = q_ref[0]

    def step(kstart, mask):
        _flash_step(q, k_ref[0, pl.ds(kstart, bq), :], v_ref[0, pl.ds(kstart, bq), :], mask,
                    m_ref, l_ref, acc_ref, 0)

    def body(jb, carry):
        step(pl.multiple_of(jb * bq, bq), None)
        return carry

    lax.fori_loop(0, i, body, 0)
    step(pl.multiple_of(i * bq, bq), _chunk_mask(bq))
    o = acc_ref[0] / l_ref[0]
    o_ref[0] = (o * gate_ref[0].astype(F32)).astype(BF16)


def _mla_attention(q3, k3, v3, gate3):
    B, S, _ = q3.shape
    bq = min(ATTN_BLOCK, S)
    H = MLA_HEADS
    kern = functools.partial(_mla_attn_kernel, bq=bq)
    return pl.pallas_call(
        kern,
        out_shape=jax.ShapeDtypeStruct((B, S, H * MLA_V), BF16),
        grid=(B, H, S // bq),
        in_specs=[
            pl.BlockSpec((1, bq, 256), lambda b, h, i: (b, i, h)),
            pl.BlockSpec((1, S, 256), lambda b, h, i: (b, 0, h)),
            pl.BlockSpec((1, S, MLA_V), lambda b, h, i: (b, 0, h)),
            pl.BlockSpec((1, bq, MLA_V), lambda b, h, i: (b, i, h)),
        ],
        out_specs=pl.BlockSpec((1, bq, MLA_V), lambda b, h, i: (b, i, h)),
        scratch_shapes=[pltpu.VMEM((1, bq, 1), F32), pltpu.VMEM((1, bq, 1), F32),
                        pltpu.VMEM((1, bq, MLA_V), F32)],
        compiler_params=_cparams("parallel", "parallel", "arbitrary"),
        name="mla_attention",
    )(q3, k3, v3, gate3)


def _lru_kernel(xb_ref, gate_ref, cw_ref, cb_ref, wa_ref, ba_ref, wx_ref, bx_ref, lam_ref,
                o_ref, ext_ref, xc_ref, a_ref, b_ref, h_ref, *, ts):
    t = pl.program_id(1)
    W = xb_ref.shape[-1]

    @pl.when(t == 0)
    def _():
        ext_ref[0:8, :] = jnp.zeros((8, W), F32)
        h_ref[...] = jnp.zeros((1, W), F32)

    ext_ref[8:8 + ts, :] = xb_ref[0]
    xc_ref[...] = (cb_ref[...]
                   + cw_ref[3:4, :] * ext_ref[8:8 + ts, :]
                   + cw_ref[2:3, :] * ext_ref[7:7 + ts, :]
                   + cw_ref[1:2, :] * ext_ref[6:6 + ts, :]
                   + cw_ref[0:1, :] * ext_ref[5:5 + ts, :])
    ext_ref[0:8, :] = ext_ref[ts:ts + 8, :]

    nl = -lam_ref[...]
    softplus = jnp.maximum(nl, 0.0) + jnp.log1p(jnp.exp(-jnp.abs(nl)))
    for g in range(LRU_BLOCKS):
        sl = slice(g * LRU_BLOCK, (g + 1) * LRU_BLOCK)
        xc = xc_ref[:, sl]
        xcb = xc.astype(BF16)
        r = jax.nn.sigmoid(jnp.dot(xcb, wa_ref[g], preferred_element_type=F32) + ba_ref[:, sl])
        ig = jax.nn.sigmoid(jnp.dot(xcb, wx_ref[g], preferred_element_type=F32) + bx_ref[:, sl])
        log_a = -LRU_C * r * softplus[:, sl]
        th = jnp.tanh(log_a)
        a_ref[:, sl] = jnp.exp(log_a)
        b_ref[:, sl] = jnp.sqrt(-2.0 * th / (1.0 - th)) * (ig * xc)

    def body(r8, h):
        base = pl.multiple_of(r8 * 8, 8)
        for r in range(8):
            h = a_ref[pl.ds(base + r, 1), :] * h + b_ref[pl.ds(base + r, 1), :]
            b_ref[pl.ds(base + r, 1), :] = h
        return h

    h_ref[...] = lax.fori_loop(0, ts // 8, body, h_ref[...])
    o_ref[0] = (b_ref[...] * gate_ref[0].astype(F32)).astype(BF16)


def _lru_mixer(xb3, gate3, conv_w, conv_b, w_a, b_a, w_x, b_x, lam):
    B, S, W = xb3.shape
    ts = min(LRU_TIME_TILE, S)
    vec = pl.BlockSpec((1, W), lambda b, t: (0, 0))
    wblk = pl.BlockSpec((LRU_BLOCKS, LRU_BLOCK, LRU_BLOCK), lambda b, t: (0, 0, 0))
    tile = pl.BlockSpec((1, ts, W), lambda b, t: (b, t, 0))
    kern = functools.partial(_lru_kernel, ts=ts)
    return pl.pallas_call(
        kern,
        out_shape=jax.ShapeDtypeStruct((B, S, W), BF16),
        grid=(B, S // ts),
        in_specs=[tile, tile, pl.BlockSpec((CONV_WIDTH, W), lambda b, t: (0, 0)), vec,
                  wblk, vec, wblk, vec, vec],
        out_specs=tile,
        scratch_shapes=[pltpu.VMEM((ts + 8, W), F32), pltpu.VMEM((ts, W), F32),
                        pltpu.VMEM((ts, W), F32), pltpu.VMEM((ts, W), F32),
                        pltpu.VMEM((1, W), F32)],
        compiler_params=_cparams("parallel", "arbitrary"),
        name="rglru",
    )(xb3, gate3, conv_w, conv_b.reshape(1, W), w_a.astype(BF16), b_a.reshape(1, W),
      w_x.astype(BF16), b_x.reshape(1, W), lam.reshape(1, W))


def _row_spec(width, col_block=None):
    if col_block is None:
        return lambda tm: pl.BlockSpec((tm, width), lambda i, j: (i, j))
    return lambda tm: pl.BlockSpec((tm, width), lambda i, j: (i, col_block))


def _const_spec(shape):
    return pl.BlockSpec(shape, lambda i, j: (0,) * len(shape))


def _diff_layer(h, B, S, tabs, gain, w_in, q_gain, k_gain, lq1, lk1, lq2, lk2, sub_gain, layer_idx):
    T = h.shape[0]
    tm = min(ROW_TILE, T)
    c_da, s_da = tabs[0], tabs[1]
    scale = DA_HEAD_DIM ** -0.5
    qk_gain = jnp.stack([q_gain * scale, k_gain]).reshape(2, 1, DA_HEAD_DIM).astype(F32)
    n_q = DA_HEADS * 2 * DA_HEAD_DIM // COL_TILE
    tab_spec = pl.BlockSpec((tm, LANES), lambda i, j: (i, 0))
    u = _projection(
        h, w_in.astype(BF16), gain=gain, epilogue=_da_in_epilogue,
        extras=[(qk_gain, pl.BlockSpec((1, 1, DA_HEAD_DIM), lambda i, j: (jnp.minimum(j // n_q, 1), 0, 0))),
                (c_da, tab_spec), (s_da, tab_spec)],
        out_shapes=[jax.ShapeDtypeStruct((T, w_in.shape[1]), BF16)],
        out_specs=[pl.BlockSpec((tm, COL_TILE), lambda i, j: (i, j))],
        name="da_in_proj")[0]
    lam_init = 0.8 - 0.6 * math.exp(-0.3 * layer_idx)
    y = _da_attention(u.reshape(B, S, -1), lq1, lk1, lq2, lk2, sub_gain, lam_init)
    return y.reshape(T, -1)


def _mla_layer(h, B, S, tabs, gain, w_in, cq_gain, ckv_gain, w_uq, w_ukv, qn_gain, qr_gain,
               kn_gain, kr_gain):
    T = h.shape[0]
    tm = min(ROW_TILE, T)
    c_m, s_m1, s_m2 = tabs[2], tabs[3], tabs[4]
    H = MLA_HEADS
    n_lat = MLA_Q_RANK + MLA_KV_RANK + MLA_ROPE
    lat_w = n_lat + (LANES - MLA_ROPE)
    w_lat = jnp.pad(w_in[:, :n_lat], ((0, 0), (0, lat_w - n_lat))).astype(BF16)
    w_gate = w_in[:, n_lat:].astype(BF16)
    lat = _projection(
        h, w_lat, gain=gain, epilogue=_f32_epilogue, tn=lat_w,
        out_shapes=[jax.ShapeDtypeStruct((T, lat_w), F32)],
        out_specs=[pl.BlockSpec((tm, lat_w), lambda i, j: (i, j))],
        name="mla_latent_proj")[0]
    gate = _projection(
        h, w_gate, gain=gain, epilogue=_silu_epilogue,
        out_shapes=[jax.ShapeDtypeStruct((T, w_gate.shape[1]), BF16)],
        out_specs=[pl.BlockSpec((tm, COL_TILE), lambda i, j: (i, j))],
        name="mla_gate_proj")[0]

    scale = (MLA_NOPE + MLA_ROPE) ** -0.5
    pad_r = lambda g: jnp.pad(g, (0, LANES - MLA_ROPE)).reshape(1, LANES).astype(F32)
    hd = MLA_NOPE + MLA_ROPE
    w_uq_p = jnp.pad(w_uq.reshape(MLA_Q_RANK, H, hd), ((0, 0), (0, 0), (0, 256 - hd)))
    w_uq_p = w_uq_p.reshape(MLA_Q_RANK, H * 256).astype(BF16)
    vec = _const_spec((1, LANES))
    tab_spec = pl.BlockSpec((tm, LANES), lambda i, j: (i, 0))
    tab_extras = [(c_m, tab_spec), (s_m1, tab_spec), (s_m2, tab_spec)]
    q = _projection(
        lat, w_uq_p, gain=cq_gain, x_col_block=0, epilogue=_mla_q_epilogue, tn=256,
        extras=[((qn_gain * scale).reshape(1, LANES), vec), (pad_r(qr_gain * scale), vec)] + tab_extras,
        out_shapes=[jax.ShapeDtypeStruct((T, H * 256), BF16)],
        out_specs=[pl.BlockSpec((tm, 256), lambda i, j: (i, j))],
        name="mla_q_proj")[0]
    kpe_block = (MLA_Q_RANK + MLA_KV_RANK) // LANES
    k, v = _projection(
        lat, w_ukv.astype(BF16), gain=ckv_gain, x_col_block=1, epilogue=_mla_kv_epilogue, tn=256,
        extras=[(kn_gain.reshape(1, LANES), vec), (pad_r(kr_gain), vec),
                (lat, pl.BlockSpec((tm, LANES), lambda i, j: (i, kpe_block)))] + tab_extras,
        out_shapes=[jax.ShapeDtypeStruct((T, H * 256), BF16), jax.ShapeDtypeStruct((T, H * MLA_V), BF16)],
        out_specs=[pl.BlockSpec((tm, 256), lambda i, j: (i, j)),
                   pl.BlockSpec((tm, MLA_V), lambda i, j: (i, j))],
        name="mla_kv_proj")
    y = _mla_attention(q.reshape(B, S, -1), k.reshape(B, S, -1), v.reshape(B, S, -1),
                       gate.reshape(B, S, -1))
    return y.reshape(T, -1)


def _lru_layer(h, B, S, gain, w_in, conv_w, conv_b, w_a, b_a, w_x, b_x, lam):
    T = h.shape[0]
    tm = min(ROW_TILE, T)
    W = D_MODEL
    xb = _projection(
        h, w_in[:, :W].astype(BF16), gain=gain, epilogue=_f32_epilogue,
        out_shapes=[jax.ShapeDtypeStruct((T, W), F32)],
        out_specs=[pl.BlockSpec((tm, COL_TILE), lambda i, j: (i, j))],
        name="lru_x_proj")[0]
    gate = _projection(
        h, w_in[:, W:].astype(BF16), gain=gain, epilogue=_silu_epilogue,
        out_shapes=[jax.ShapeDtypeStruct((T, W), BF16)],
        out_specs=[pl.BlockSpec((tm, COL_TILE), lambda i, j: (i, j))],
        name="lru_gate_proj")[0]
    y = _lru_mixer(xb.reshape(B, S, W), gate.reshape(B, S, W), conv_w, conv_b, w_a, b_a, w_x, b_x, lam)
    return y.reshape(T, W)


def _post_mixer(h, y, w_out, p_i, ple_gain, w_gate, w_proj):
    T = h.shape[0]
    tm = min(ROW_TILE, T)
    tile = pl.BlockSpec((tm, COL_TILE), lambda i, j: (i, j))
    out = [jax.ShapeDtypeStruct((T, D_MODEL), F32)]
    h = _projection(y, w_out.astype(BF16), epilogue=_residual_epilogue, extras=[(h, tile)],
                    out_shapes=out, out_specs=[tile], name="out_proj")[0]
    P = p_i.shape[1]
    return _projection(
        h, w_gate.astype(BF16), gain=ple_gain, epilogue=_ple_epilogue,
        extras=[(h, tile), (p_i, pl.BlockSpec((tm, P), lambda i, j: (i, 0))),
                (w_proj.astype(BF16), pl.BlockSpec((P, COL_TILE), lambda i, j: (0, j)))],
        out_shapes=out, out_specs=[tile], name="ple")[0]


def kernel(x, p, positions, norm_gain, a_w_in, a_q_norm, a_k_norm, a_lambda_q1, a_lambda_k1, a_lambda_q2, a_lambda_k2, a_sub_norm, a_w_out, b_w_in, b_cq_norm, b_ckv_norm, b_w_uq, b_w_ukv, b_q_nope_norm, b_q_rope_norm, b_k_nope_norm, b_k_rope_norm, b_w_out, c_w_in, c_conv_w, c_conv_b, c_w_a, c_b_a, c_w_x, c_b_x, c_lambda, c_w_out, ple_norm, ple_w_gate, ple_w_proj):
    B, S, D = x.shape
    T = B * S
    depth = p.shape[0]
    h = x.reshape(T, D)
    tabs = _rope_tables(positions.reshape(T, 1))
    for i in range(depth):
        j = i // N_MIXERS
        kind = i % N_MIXERS
        if kind == 0:
            y = _diff_layer(h, B, S, tabs, norm_gain[i], a_w_in[j], a_q_norm[j], a_k_norm[j],
                            a_lambda_q1[j], a_lambda_k1[j], a_lambda_q2[j], a_lambda_k2[j],
                            a_sub_norm[j], i)
            w_out = a_w_out[j]
        elif kind == 1:
            y = _mla_layer(h, B, S, tabs, norm_gain[i], b_w_in[j], b_cq_norm[j], b_ckv_norm[j],
                           b_w_uq[j], b_w_ukv[j], b_q_nope_norm[j], b_q_rope_norm[j],
                           b_k_nope_norm[j], b_k_rope_norm[j])
            w_out = b_w_out[j]
        else:
            y = _lru_layer(h, B, S, norm_gain[i], c_w_in[j], c_conv_w[j], c_conv_b[j], c_w_a[j],
                           c_b_a[j], c_w_x[j], c_b_x[j], c_lambda[j])
            w_out = c_w_out[j]
        h = _post_mixer(h, y, w_out, p[i].reshape(T, -1), ple_norm[i], ple_w_gate[i], ple_w_proj[i])
    return h.reshape(B, S, D)
```

```python
import functools
import math

import jax
import jax.numpy as jnp
from jax import lax
from jax.experimental import pallas as pl
from jax.experimental.pallas import tpu as pltpu

F32 = jnp.float32
BF16 = jnp.bfloat16

D_MODEL = 2048
CHUNK = 64
ROPE_THETA = 10000.0
EPS = 1e-6
N_MIXERS = 3

DA_HEAD_DIM = 128
DA_V_DIM = 256
DA_HEADS = 8
MLA_NOPE = 128
MLA_ROPE = 64
MLA_V = 128
MLA_HEADS = 16
MLA_Q_RANK = 512
MLA_KV_RANK = 512
LRU_BLOCKS = 8
LRU_BLOCK = 256
CONV_WIDTH = 4
LRU_C = 8.0

LANES = 128
VMEM_LIMIT = 56 * 1024 * 1024
ROW_TILE = 1024
COL_TILE = 512
ATTN_BLOCK = 512
LRU_TIME_TILE = 256


def _cparams(*sem):
    return pltpu.CompilerParams(dimension_semantics=sem, vmem_limit_bytes=VMEM_LIMIT)


def _rope_table_kernel(pos_ref, inv_da_ref, inv_mla_ref, c_da, s_da, c_m, s_m1, s_m2):
    pos = pos_ref[...].astype(F32)
    lane = lax.broadcasted_iota(jnp.int32, c_da.shape, 1)
    ang = pos * inv_da_ref[...]
    sn = jnp.sin(ang)
    c_da[...] = jnp.cos(ang)
    s_da[...] = jnp.where(lane < 64, -sn, sn)
    ang = pos * inv_mla_ref[...]
    cs = jnp.cos(ang)
    sn = jnp.sin(ang)
    c_m[...] = jnp.where(lane < 64, cs, 0.0)
    s_m1[...] = jnp.where(lane < 32, -sn, 0.0)
    s_m2[...] = jnp.where((lane >= 32) & (lane < 64), sn, 0.0)


def _rope_tables(pos_col):
    T = pos_col.shape[0]
    tr = min(2048, T)
    half = DA_HEAD_DIM // 2
    inv = ROPE_THETA ** (-jnp.arange(half, dtype=F32) / half)
    inv_da = jnp.concatenate([inv, inv])[None, :]
    half = MLA_ROPE // 2
    inv = ROPE_THETA ** (-jnp.arange(half, dtype=F32) / half)
    inv_mla = jnp.concatenate([inv, inv, inv, inv])[None, :]
    tab = jax.ShapeDtypeStruct((T, LANES), F32)
    row = pl.BlockSpec((tr, LANES), lambda i: (i, 0))
    vec = pl.BlockSpec((1, LANES), lambda i: (0, 0))
    return pl.pallas_call(
        _rope_table_kernel,
        out_shape=[tab] * 5,
        grid=(T // tr,),
        in_specs=[pl.BlockSpec((tr, 1), lambda i: (i, 0)), vec, vec],
        out_specs=[row] * 5,
        compiler_params=_cparams("parallel"),
        name="rope_tables",
    )(pos_col, inv_da, inv_mla)


def _proj_kernel(x_ref, g_ref, w_ref, *rest, n_extra, n_out, norm, epilogue):
    extra = rest[:n_extra]
    outs = rest[n_extra:n_extra + n_out]
    j = pl.program_id(1)
    if norm:
        xn_ref = rest[-1]

        @pl.when(j == 0)
        def _():
            x = x_ref[...].astype(F32)
            ms = jnp.mean(x * x, axis=-1, keepdims=True)
            xn_ref[...] = (x * lax.rsqrt(ms + EPS) * g_ref[...]).astype(BF16)

        lhs = xn_ref[...]
    else:
        lhs = x_ref[...]
    acc = jnp.dot(lhs, w_ref[...], preferred_element_type=F32)
    epilogue(acc, j, extra, outs)


def _projection(x, w, *, epilogue, out_shapes, out_specs, gain=None, x_col_block=0,
                extras=(), tm=ROW_TILE, tn=COL_TILE, name):
    T = x.shape[0]
    K, N = w.shape
    tm = min(tm, T)
    tn = min(tn, N)
    norm = gain is not None
    if not norm:
        gain = jnp.zeros((1, K), F32)
    in_specs = [
        pl.BlockSpec((tm, K), lambda i, j: (i, x_col_block)),
        pl.BlockSpec((1, K), lambda i, j: (0, 0)),
        pl.BlockSpec((K, tn), lambda i, j: (0, j)),
    ] + [spec for _, spec in extras]
    kern = functools.partial(_proj_kernel, n_extra=len(extras), n_out=len(out_shapes),
                             norm=norm, epilogue=epilogue)
    return pl.pallas_call(
        kern,
        out_shape=out_shapes,
        grid=(T // tm, N // tn),
        in_specs=in_specs,
        out_specs=out_specs,
        scratch_shapes=[pltpu.VMEM((tm, K), BF16)] if norm else [],
        compiler_params=_cparams("parallel", "arbitrary"),
        name=name,
    )(x, gain.reshape(1, K).astype(F32), w, *[a for a, _ in extras])


def _rms(x, width):
    return x * lax.rsqrt(jnp.sum(x * x, axis=-1, keepdims=True) * (1.0 / width) + EPS)


def _silu(x):
    return x * jax.nn.sigmoid(x)


def _da_in_epilogue(acc, j, extra, outs):
    g_ref, c_ref, s_ref = extra
    (o_ref,) = outs
    n_qk = 2 * DA_HEADS * 2 * DA_HEAD_DIM // COL_TILE
    n_v = DA_HEADS * DA_V_DIM // COL_TILE

    @pl.when(j < n_qk)
    def _():
        g = g_ref[0]
        c = c_ref[...]
        s = s_ref[...]
        for t in range(COL_TILE // DA_HEAD_DIM):
            sl = slice(t * DA_HEAD_DIM, (t + 1) * DA_HEAD_DIM)
            xn = _rms(acc[:, sl], DA_HEAD_DIM) * g
            o_ref[:, sl] = (xn * c + pltpu.roll(xn, 64, 1) * s).astype(BF16)

    @pl.when((j >= n_qk) & (j < n_qk + n_v))
    def _():
        o_ref[...] = acc.astype(BF16)

    @pl.when(j >= n_qk + n_v)
    def _():
        o_ref[...] = _silu(acc).astype(BF16)


def _silu_epilogue(acc, j, extra, outs):
    outs[0][...] = _silu(acc).astype(BF16)


def _f32_epilogue(acc, j, extra, outs):
    outs[0][...] = acc


def _mla_rope(zn, c_m, s_m1, s_m2):
    return zn * c_m + pltpu.roll(zn, 96, 1) * s_m1 + pltpu.roll(zn, 32, 1) * s_m2


def _mla_q_epilogue(acc, j, extra, outs):
    gn_ref, gr_ref, c_ref, s1_ref, s2_ref = extra
    (o_ref,) = outs
    o_ref[:, :MLA_NOPE] = (_rms(acc[:, :MLA_NOPE], MLA_NOPE) * gn_ref[...]).astype(BF16)
    zn = _rms(acc[:, MLA_NOPE:], MLA_ROPE) * gr_ref[...]
    o_ref[:, MLA_NOPE:] = _mla_rope(zn, c_ref[...], s1_ref[...], s2_ref[...]).astype(BF16)


def _mla_kv_epilogue(acc, j, extra, outs):
    gn_ref, gr_ref, kpe_ref, c_ref, s1_ref, s2_ref = extra
    k_ref, v_ref = outs
    k_ref[:, :MLA_NOPE] = (_rms(acc[:, :MLA_NOPE], MLA_NOPE) * gn_ref[...]).astype(BF16)
    zn = _rms(kpe_ref[...], MLA_ROPE) * gr_ref[...]
    k_ref[:, MLA_NOPE:] = _mla_rope(zn, c_ref[...], s1_ref[...], s2_ref[...]).astype(BF16)
    v_ref[...] = acc[:, MLA_NOPE:].astype(BF16)


def _residual_epilogue(acc, j, extra, outs):
    outs[0][...] = extra[0][...] + acc


def _ple_epilogue(acc, j, extra, outs):
    h_ref, p_ref, wp_ref = extra
    proj = jnp.dot(p_ref[...].astype(BF16), wp_ref[...], preferred_element_type=F32)
    outs[0][...] = h_ref[...] + jax.nn.sigmoid(acc) * proj


def _chunk_mask(bq, bk):
    r = lax.broadcasted_iota(jnp.int32, (bq, bk), 0) // CHUNK
    c = lax.broadcasted_iota(jnp.int32, (bq, bk), 1) // CHUNK
    return c <= r


def _lane_fold(x, op):
    out = x[:, :LANES]
    for t in range(1, x.shape[1] // LANES):
        out = op(out, x[:, t * LANES:(t + 1) * LANES])
    return out


def _scores_phase(qs, k_ref, kcols, s_ref, m_ref, i, bq):
    ncomp = len(qs)

    def block(j, mask):
        rows = pl.ds(pl.multiple_of(j * bq, bq), bq)
        folded = []
        for c in range(ncomp):
            s = lax.dot_general(qs[c], k_ref[0, rows, kcols[c]], (((1,), (1,)), ((), ())),
                                preferred_element_type=F32)
            if mask is not None:
                s = jnp.where(mask, s, -jnp.inf)
            s_ref[c, j] = s
            folded.append(_lane_fold(s, jnp.maximum))
        return folded

    for c, f in enumerate(block(i, _chunk_mask(bq, bq))):
        m_ref[c] = f

    def body(j, carry):
        for c, f in enumerate(block(j, None)):
            m_ref[c] = jnp.maximum(m_ref[c], f)
        return carry

    lax.fori_loop(0, i, body, 0)
    for c in range(ncomp):
        m_ref[c] = jnp.broadcast_to(jnp.max(m_ref[c], axis=-1, keepdims=True), (bq, LANES))


def _values_phase(v_ref, s_ref, m_ref, l_ref, acc_ref, ncomp, i, bq):
    def block(j, first):
        v = v_ref[0, pl.ds(pl.multiple_of(j * bq, bq), bq), :]
        for c in range(ncomp):
            m = jnp.concatenate([m_ref[c]] * (bq // LANES), axis=1)
            p = jnp.exp(s_ref[c, j] - m)
            pv = jnp.dot(p.astype(BF16), v, preferred_element_type=F32)
            lp = _lane_fold(p, jnp.add)
            if first:
                l_ref[c] = lp
                acc_ref[c] = pv
            else:
                l_ref[c] += lp
                acc_ref[c] += pv

    block(i, True)

    def body(j, carry):
        block(j, False)
        return carry

    lax.fori_loop(0, i, body, 0)


def _da_attn_kernel(q_ref, k_ref, v_ref, gate_ref, lq1_ref, lk1_ref, lq2_ref, lk2_ref, sg_ref,
                    o_ref, s_ref, m_ref, l_ref, acc_ref, *, bq, lam_init):
    i = pl.program_id(2)
    d = DA_HEAD_DIM
    q = q_ref[0]
    cols = [slice(0, d), slice(d, 2 * d)]
    _scores_phase([q[:, c] for c in cols], k_ref, cols, s_ref, m_ref, i, bq)
    _values_phase(v_ref, s_ref, m_ref, l_ref, acc_ref, 2, i, bq)
    lam = (jnp.exp(jnp.sum(lq1_ref[...] * lk1_ref[...], axis=-1, keepdims=True))
           - jnp.exp(jnp.sum(lq2_ref[...] * lk2_ref[...], axis=-1, keepdims=True)) + lam_init)
    l1 = jnp.sum(l_ref[0], axis=-1, keepdims=True)
    l2 = jnp.sum(l_ref[1], axis=-1, keepdims=True)
    o = acc_ref[0] / l1 - lam * (acc_ref[1] / l2)
    o = _rms(o, DA_V_DIM) * sg_ref[...] * (1.0 - lam_init)
    o_ref[0] = (o * gate_ref[0].astype(F32)).astype(BF16)


def _attn_scratch(ncomp, S, bq, dv):
    return [pltpu.VMEM((ncomp, S // bq, bq, bq), F32), pltpu.VMEM((ncomp, bq, LANES), F32),
            pltpu.VMEM((ncomp, bq, LANES), F32), pltpu.VMEM((ncomp, bq, dv), F32)]


def _da_attention(u3, lq1, lk1, lq2, lk2, sub_gain, lam_init):
    B, S, _ = u3.shape
    bq = min(ATTN_BLOCK, S)
    H = DA_HEADS
    vec = lambda n: pl.BlockSpec((1, n), lambda b, h, i: (0, 0))
    kern = functools.partial(_da_attn_kernel, bq=bq, lam_init=lam_init)
    return pl.pallas_call(
        kern,
        out_shape=jax.ShapeDtypeStruct((B, S, H * DA_V_DIM), BF16),
        grid=(B, H, S // bq),
        in_specs=[
            pl.BlockSpec((1, bq, 256), lambda b, h, i: (b, i, h)),
            pl.BlockSpec((1, S, 256), lambda b, h, i: (b, 0, H + h)),
            pl.BlockSpec((1, S, 256), lambda b, h, i: (b, 0, 2 * H + h)),
            pl.BlockSpec((1, bq, 256), lambda b, h, i: (b, i, 3 * H + h)),
            vec(128), vec(128), vec(128), vec(128), vec(256),
        ],
        out_specs=pl.BlockSpec((1, bq, 256), lambda b, h, i: (b, i, h)),
        scratch_shapes=_attn_scratch(2, S, bq, DA_V_DIM),
        compiler_params=_cparams("parallel", "parallel", "arbitrary"),
        name="da_attention",
    )(u3, u3, u3, u3, lq1.reshape(1, 128), lk1.reshape(1, 128), lq2.reshape(1, 128),
      lk2.reshape(1, 128), sub_gain.reshape(1, 256))


def _mla_attn_kernel(q_ref, k_ref, v_ref, gate_ref, o_ref, s_ref, m_ref, l_ref, acc_ref, *, bq):
    i = pl.program_id(2)
    _scores_phase([q_ref[0]], k_ref, [slice(None)], s_ref, m_ref, i, bq)
    _values_phase(v_ref, s_ref, m_ref, l_ref, acc_ref, 1, i, bq)
    o = acc_ref[0] / jnp.sum(l_ref[0], axis=-1, keepdims=True)
    o_ref[0] = (o * gate_ref[0].astype(F32)).astype(BF16)


def _mla_attention(q3, k3, v3, gate3):
    B, S, _ = q3.shape
    bq = min(ATTN_BLOCK, S)
    H = MLA_HEADS
    kern = functools.partial(_mla_attn_kernel, bq=bq)
    return pl.pallas_call(
        kern,
        out_shape=jax.ShapeDtypeStruct((B, S, H * MLA_V), BF16),
        grid=(B, H, S // bq),
        in_specs=[
            pl.BlockSpec((1, bq, 256), lambda b, h, i: (b, i, h)),
            pl.BlockSpec((1, S, 256), lambda b, h, i: (b, 0, h)),
            pl.BlockSpec((1, S, MLA_V), lambda b, h, i: (b, 0, h)),
            pl.BlockSpec((1, bq, MLA_V), lambda b, h, i: (b, i, h)),
        ],
        out_specs=pl.BlockSpec((1, bq, MLA_V), lambda b, h, i: (b, i, h)),
        scratch_shapes=_attn_scratch(1, S, bq, MLA_V),
        compiler_params=_cparams("parallel", "parallel", "arbitrary"),
        name="mla_attention",
    )(q3, k3, v3, gate3)


def _lru_kernel(xb_ref, gate_ref, cw_ref, cb_ref, wa_ref, ba_ref, wx_ref, bx_ref, lam_ref,
                o_ref, ext_ref, xc_ref, a_ref, b_ref, h_ref, *, ts):
    t = pl.program_id(1)
    W = xb_ref.shape[-1]

    @pl.when(t == 0)
    def _():
        ext_ref[0:8, :] = jnp.zeros((8, W), F32)
        h_ref[...] = jnp.zeros((1, W), F32)

    ext_ref[8:8 + ts, :] = xb_ref[0]
    xc_ref[...] = (cb_ref[...]
                   + cw_ref[3:4, :] * ext_ref[8:8 + ts, :]
                   + cw_ref[2:3, :] * ext_ref[7:7 + ts, :]
                   + cw_ref[1:2, :] * ext_ref[6:6 + ts, :]
                   + cw_ref[0:1, :] * ext_ref[5:5 + ts, :])
    ext_ref[0:8, :] = ext_ref[ts:ts + 8, :]

    nl = -lam_ref[...]
    softplus = jnp.maximum(nl, 0.0) + jnp.log1p(jnp.exp(-jnp.abs(nl)))
    for g in range(LRU_BLOCKS):
        sl = slice(g * LRU_BLOCK, (g + 1) * LRU_BLOCK)
        xc = xc_ref[:, sl]
        xcb = xc.astype(BF16)
        r = jax.nn.sigmoid(jnp.dot(xcb, wa_ref[g], preferred_element_type=F32) + ba_ref[:, sl])
        ig = jax.nn.sigmoid(jnp.dot(xcb, wx_ref[g], preferred_element_type=F32) + bx_ref[:, sl])
        log_a = -LRU_C * r * softplus[:, sl]
        th = jnp.tanh(log_a)
        a_ref[:, sl] = jnp.exp(log_a)
        b_ref[:, sl] = jnp.sqrt(-2.0 * th / (1.0 - th)) * (ig * xc)

    def body(r8, h):
        base = pl.multiple_of(r8 * 8, 8)
        for r in range(8):
            h = a_ref[pl.ds(base + r, 1), :] * h + b_ref[pl.ds(base + r, 1), :]
            b_ref[pl.ds(base + r, 1), :] = h
        return h

    h_ref[...] = lax.fori_loop(0, ts // 8, body, h_ref[...])
    o_ref[0] = (b_ref[...] * gate_ref[0].astype(F32)).astype(BF16)


def _lru_mixer(xb3, gate3, conv_w, conv_b, w_a, b_a, w_x, b_x, lam):
    B, S, W = xb3.shape
    ts = min(LRU_TIME_TILE, S)
    vec = pl.BlockSpec((1, W), lambda b, t: (0, 0))
    wblk = pl.BlockSpec((LRU_BLOCKS, LRU_BLOCK, LRU_BLOCK), lambda b, t: (0, 0, 0))
    tile = pl.BlockSpec((1, ts, W), lambda b, t: (b, t, 0))
    kern = functools.partial(_lru_kernel, ts=ts)
    return pl.pallas_call(
        kern,
        out_shape=jax.ShapeDtypeStruct((B, S, W), BF16),
        grid=(B, S // ts),
        in_specs=[tile, tile, pl.BlockSpec((CONV_WIDTH, W), lambda b, t: (0, 0)), vec,
                  wblk, vec, wblk, vec, vec],
        out_specs=tile,
        scratch_shapes=[pltpu.VMEM((ts + 8, W), F32), pltpu.VMEM((ts, W), F32),
                        pltpu.VMEM((ts, W), F32), pltpu.VMEM((ts, W), F32),
                        pltpu.VMEM((1, W), F32)],
        compiler_params=_cparams("parallel", "arbitrary"),
        name="rglru",
    )(xb3, gate3, conv_w, conv_b.reshape(1, W), w_a.astype(BF16), b_a.reshape(1, W),
      w_x.astype(BF16), b_x.reshape(1, W), lam.reshape(1, W))


def _row_spec(width, col_block=None):
    if col_block is None:
        return lambda tm: pl.BlockSpec((tm, width), lambda i, j: (i, j))
    return lambda tm: pl.BlockSpec((tm, width), lambda i, j: (i, col_block))


def _const_spec(shape):
    return pl.BlockSpec(shape, lambda i, j: (0,) * len(shape))


def _diff_layer(h, B, S, tabs, gain, w_in, q_gain, k_gain, lq1, lk1, lq2, lk2, sub_gain, layer_idx):
    T = h.shape[0]
    tm = min(ROW_TILE, T)
    c_da, s_da = tabs[0], tabs[1]
    scale = DA_HEAD_DIM ** -0.5
    qk_gain = jnp.stack([q_gain * scale, k_gain]).reshape(2, 1, DA_HEAD_DIM).astype(F32)
    n_q = DA_HEADS * 2 * DA_HEAD_DIM // COL_TILE
    tab_spec = pl.BlockSpec((tm, LANES), lambda i, j: (i, 0))
    u = _projection(
        h, w_in.astype(BF16), gain=gain, epilogue=_da_in_epilogue,
        extras=[(qk_gain, pl.BlockSpec((1, 1, DA_HEAD_DIM), lambda i, j: (jnp.minimum(j // n_q, 1), 0, 0))),
                (c_da, tab_spec), (s_da, tab_spec)],
        out_shapes=[jax.ShapeDtypeStruct((T, w_in.shape[1]), BF16)],
        out_specs=[pl.BlockSpec((tm, COL_TILE), lambda i, j: (i, j))],
        name="da_in_proj")[0]
    lam_init = 0.8 - 0.6 * math.exp(-0.3 * layer_idx)
    y = _da_attention(u.reshape(B, S, -1), lq1, lk1, lq2, lk2, sub_gain, lam_init)
    return y.reshape(T, -1)


def _mla_layer(h, B, S, tabs, gain, w_in, cq_gain, ckv_gain, w_uq, w_ukv, qn_gain, qr_gain,
               kn_gain, kr_gain):
    T = h.shape[0]
    tm = min(ROW_TILE, T)
    c_m, s_m1, s_m2 = tabs[2], tabs[3], tabs[4]
    H = MLA_HEADS
    n_lat = MLA_Q_RANK + MLA_KV_RANK + MLA_ROPE
    lat_w = n_lat + (LANES - MLA_ROPE)
    w_lat = jnp.pad(w_in[:, :n_lat], ((0, 0), (0, lat_w - n_lat))).astype(BF16)
    w_gate = w_in[:, n_lat:].astype(BF16)
    lat = _projection(
        h, w_lat, gain=gain, epilogue=_f32_epilogue, tn=lat_w,
        out_shapes=[jax.ShapeDtypeStruct((T, lat_w), F32)],
        out_specs=[pl.BlockSpec((tm, lat_w), lambda i, j: (i, j))],
        name="mla_latent_proj")[0]
    gate = _projection(
        h, w_gate, gain=gain, epilogue=_silu_epilogue,
        out_shapes=[jax.ShapeDtypeStruct((T, w_gate.shape[1]), BF16)],
        out_specs=[pl.BlockSpec((tm, COL_TILE), lambda i, j: (i, j))],
        name="mla_gate_proj")[0]

    scale = (MLA_NOPE + MLA_ROPE) ** -0.5
    pad_r = lambda g: jnp.pad(g, (0, LANES - MLA_ROPE)).reshape(1, LANES).astype(F32)
    hd = MLA_NOPE + MLA_ROPE
    w_uq_p = jnp.pad(w_uq.reshape(MLA_Q_RANK, H, hd), ((0, 0), (0, 0), (0, 256 - hd)))
    w_uq_p = w_uq_p.reshape(MLA_Q_RANK, H * 256).astype(BF16)
    vec = _const_spec((1, LANES))
    tab_spec = pl.BlockSpec((tm, LANES), lambda i, j: (i, 0))
    tab_extras = [(c_m, tab_spec), (s_m1, tab_spec), (s_m2, tab_spec)]
    q = _projection(
        lat, w_uq_p, gain=cq_gain, x_col_block=0, epilogue=_mla_q_epilogue, tn=256,
        extras=[((qn_gain * scale).reshape(1, LANES), vec), (pad_r(qr_gain * scale), vec)] + tab_extras,
        out_shapes=[jax.ShapeDtypeStruct((T, H * 256), BF16)],
        out_specs=[pl.BlockSpec((tm, 256), lambda i, j: (i, j))],
        name="mla_q_proj")[0]
    kpe_block = (MLA_Q_RANK + MLA_KV_RANK) // LANES
    k, v = _projection(
        lat, w_ukv.astype(BF16), gain=ckv_gain, x_col_block=1, epilogue=_mla_kv_epilogue, tn=256,
        extras=[(kn_gain.reshape(1, LANES), vec), (pad_r(kr_gain), vec),
                (lat, pl.BlockSpec((tm, LANES), lambda i, j: (i, kpe_block)))] + tab_extras,
        out_shapes=[jax.ShapeDtypeStruct((T, H * 256), BF16), jax.ShapeDtypeStruct((T, H * MLA_V), BF16)],
        out_specs=[pl.BlockSpec((tm, 256), lambda i, j: (i, j)),
                   pl.BlockSpec((tm, MLA_V), lambda i, j: (i, j))],
        name="mla_kv_proj")
    y = _mla_attention(q.reshape(B, S, -1), k.reshape(B, S, -1), v.reshape(B, S, -1),
                       gate.reshape(B, S, -1))
    return y.reshape(T, -1)


def _lru_layer(h, B, S, gain, w_in, conv_w, conv_b, w_a, b_a, w_x, b_x, lam):
    T = h.shape[0]
    tm = min(ROW_TILE, T)
    W = D_MODEL
    xb = _projection(
        h, w_in[:, :W].astype(BF16), gain=gain, epilogue=_f32_epilogue,
        out_shapes=[jax.ShapeDtypeStruct((T, W), F32)],
        out_specs=[pl.BlockSpec((tm, COL_TILE), lambda i, j: (i, j))],
        name="lru_x_proj")[0]
    gate = _projection(
        h, w_in[:, W:].astype(BF16), gain=gain, epilogue=_silu_epilogue,
        out_shapes=[jax.ShapeDtypeStruct((T, W), BF16)],
        out_specs=[pl.BlockSpec((tm, COL_TILE), lambda i, j: (i, j))],
        name="lru_gate_proj")[0]
    y = _lru_mixer(xb.reshape(B, S, W), gate.reshape(B, S, W), conv_w, conv_b, w_a, b_a, w_x, b_x, lam)
    return y.reshape(T, W)


def _post_mixer(h, y, w_out, p_i, ple_gain, w_gate, w_proj):
    T = h.shape[0]
    tm = min(ROW_TILE, T)
    tile = pl.BlockSpec((tm, COL_TILE), lambda i, j: (i, j))
    out = [jax.ShapeDtypeStruct((T, D_MODEL), F32)]
    h = _projection(y, w_out.astype(BF16), epilogue=_residual_epilogue, extras=[(h, tile)],
                    out_shapes=out, out_specs=[tile], name="out_proj")[0]
    P = p_i.shape[1]
    return _projection(
        h, w_gate.astype(BF16), gain=ple_gain, epilogue=_ple_epilogue,
        extras=[(h, tile), (p_i, pl.BlockSpec((tm, P), lambda i, j: (i, 0))),
                (w_proj.astype(BF16), pl.BlockSpec((P, COL_TILE), lambda i, j: (0, j)))],
        out_shapes=out, out_specs=[tile], name="ple")[0]


def kernel(x, p, positions, norm_gain, a_w_in, a_q_norm, a_k_norm, a_lambda_q1, a_lambda_k1, a_lambda_q2, a_lambda_k2, a_sub_norm, a_w_out, b_w_in, b_cq_norm, b_ckv_norm, b_w_uq, b_w_ukv, b_q_nope_norm, b_q_rope_norm, b_k_nope_norm, b_k_rope_norm, b_w_out, c_w_in, c_conv_w, c_conv_b, c_w_a, c_b_a, c_w_x, c_b_x, c_lambda, c_w_out, ple_norm, ple_w_gate, ple_w_proj):
    B, S, D = x.shape
    T = B * S
    depth = p.shape[0]
    h = x.reshape(T, D)
    tabs = _rope_tables(positions.reshape(T, 1))
    for i in range(depth):
        j = i // N_MIXERS
        kind = i % N_MIXERS
        if kind == 0:
            y = _diff_layer(h, B, S, tabs, norm_gain[i], a_w_in[j], a_q_norm[j], a_k_norm[j],
                            a_lambda_q1[j], a_lambda_k1[j], a_lambda_q2[j], a_lambda_k2[j],
                            a_sub_norm[j], i)
            w_out = a_w_out[j]
        elif kind == 1:
            y = _mla_layer(h, B, S, tabs, norm_gain[i], b_w_in[j], b_cq_norm[j], b_ckv_norm[j],
                           b_w_uq[j], b_w_ukv[j], b_q_nope_norm[j], b_q_rope_norm[j],
                           b_k_nope_norm[j], b_k_rope_norm[j])
            w_out = b_w_out[j]
        else:
            y = _lru_layer(h, B, S, norm_gain[i], c_w_in[j], c_conv_w[j], c_conv_b[j], c_w_a[j],
                           c_b_a[j], c_w_x[j], c_b_x[j], c_lambda[j])
            w_out = c_w_out[j]
        h = _post_mixer(h, y, w_out, p[i].reshape(T, -1), ple_norm[i], ple_w_gate[i], ple_w_proj[i])
    return h.reshape(B, S, D)
```

```python
import functools
import math

import jax
import jax.numpy as jnp
from jax import lax
from jax.experimental import pallas as pl
from jax.experimental.pallas import tpu as pltpu

F32 = jnp.float32
BF16 = jnp.bfloat16

D_MODEL = 2048
CHUNK = 64
ROPE_THETA = 10000.0
EPS = 1e-6
N_MIXERS = 3

DA_HEAD_DIM = 128
DA_V_DIM = 256
DA_HEADS = 8
MLA_NOPE = 128
MLA_ROPE = 64
MLA_V = 128
MLA_HEADS = 16
MLA_Q_RANK = 512
MLA_KV_RANK = 512
LRU_BLOCKS = 8
LRU_BLOCK = 256
CONV_WIDTH = 4
LRU_C = 8.0

LANES = 128
VMEM_LIMIT = 56 * 1024 * 1024
ROW_TILE = 1024
COL_TILE = 1024
PLE_COL_TILE = 512
MLA_HEAD_TILE = 512
LOG2E = math.log2(math.e)
ATTN_BLOCK = 512
LRU_TIME_TILE = 256


def _cparams(*sem):
    return pltpu.CompilerParams(dimension_semantics=sem, vmem_limit_bytes=VMEM_LIMIT)


def _rope_table_kernel(pos_ref, inv_da_ref, inv_mla_ref, c_da, s_da, c_m, s_m1, s_m2):
    pos = pos_ref[...].astype(F32)
    lane = lax.broadcasted_iota(jnp.int32, c_da.shape, 1)
    ang = pos * inv_da_ref[...]
    sn = jnp.sin(ang)
    c_da[...] = jnp.cos(ang)
    s_da[...] = jnp.where(lane < 64, -sn, sn)
    ang = pos * inv_mla_ref[...]
    cs = jnp.cos(ang)
    sn = jnp.sin(ang)
    c_m[...] = jnp.where(lane < 64, cs, 0.0)
    s_m1[...] = jnp.where(lane < 32, -sn, 0.0)
    s_m2[...] = jnp.where((lane >= 32) & (lane < 64), sn, 0.0)


def _rope_tables(pos_col):
    T = pos_col.shape[0]
    tr = min(2048, T)
    half = DA_HEAD_DIM // 2
    inv = ROPE_THETA ** (-jnp.arange(half, dtype=F32) / half)
    inv_da = jnp.concatenate([inv, inv])[None, :]
    half = MLA_ROPE // 2
    inv = ROPE_THETA ** (-jnp.arange(half, dtype=F32) / half)
    inv_mla = jnp.concatenate([inv, inv, inv, inv])[None, :]
    tab = jax.ShapeDtypeStruct((T, LANES), F32)
    row = pl.BlockSpec((tr, LANES), lambda i: (i, 0))
    vec = pl.BlockSpec((1, LANES), lambda i: (0, 0))
    return pl.pallas_call(
        _rope_table_kernel,
        out_shape=[tab] * 5,
        grid=(T // tr,),
        in_specs=[pl.BlockSpec((tr, 1), lambda i: (i, 0)), vec, vec],
        out_specs=[row] * 5,
        compiler_params=_cparams("parallel"),
        name="rope_tables",
    )(pos_col, inv_da, inv_mla)


def _proj_kernel(x_ref, g_ref, w_ref, *rest, n_extra, n_out, norm, epilogue):
    extra = rest[:n_extra]
    outs = rest[n_extra:n_extra + n_out]
    scratch = rest[n_extra + n_out:]
    j = pl.program_id(1)
    if norm:
        xn_ref, scratch = scratch[0], scratch[1:]

        @pl.when(j == 0)
        def _():
            x = x_ref[...].astype(F32)
            ms = jnp.mean(x * x, axis=-1, keepdims=True)
            xn_ref[...] = (x * lax.rsqrt(ms + EPS) * g_ref[...]).astype(BF16)

        lhs = xn_ref[...]
    else:
        lhs = x_ref[...]
    acc = jnp.dot(lhs, w_ref[...], preferred_element_type=F32)
    epilogue(acc, j, extra, outs, *scratch)


def _projection(x, w, *, epilogue, out_shapes, out_specs, gain=None, x_col_block=0,
                extras=(), scratch=(), tm=ROW_TILE, tn=COL_TILE, name):
    T = x.shape[0]
    K, N = w.shape
    tm = min(tm, T)
    tn = min(tn, N)
    norm = gain is not None
    if not norm:
        gain = jnp.zeros((1, K), F32)
    in_specs = [
        pl.BlockSpec((tm, K), lambda i, j: (i, x_col_block)),
        pl.BlockSpec((1, K), lambda i, j: (0, 0)),
        pl.BlockSpec((K, tn), lambda i, j: (0, j)),
    ] + [spec for _, spec in extras]
    kern = functools.partial(_proj_kernel, n_extra=len(extras), n_out=len(out_shapes),
                             norm=norm, epilogue=epilogue)
    return pl.pallas_call(
        kern,
        out_shape=out_shapes,
        grid=(T // tm, N // tn),
        in_specs=in_specs,
        out_specs=out_specs,
        scratch_shapes=([pltpu.VMEM((tm, K), BF16)] if norm else []) + list(scratch),
        compiler_params=_cparams("parallel", "arbitrary"),
        name=name,
    )(x, gain.reshape(1, K).astype(F32), w, *[a for a, _ in extras])


def _rms(x, width):
    return x * lax.rsqrt(jnp.sum(x * x, axis=-1, keepdims=True) * (1.0 / width) + EPS)


def _silu(x):
    return x * jax.nn.sigmoid(x)


def _da_in_epilogue(acc, j, extra, outs):
    g_ref, c_ref, s_ref = extra
    (o_ref,) = outs
    n_qk = 2 * DA_HEADS * 2 * DA_HEAD_DIM // COL_TILE
    n_v = DA_HEADS * DA_V_DIM // COL_TILE

    @pl.when(j < n_qk)
    def _():
        g = g_ref[0]
        c = c_ref[...]
        s = s_ref[...]
        for t in range(COL_TILE // DA_HEAD_DIM):
            sl = slice(t * DA_HEAD_DIM, (t + 1) * DA_HEAD_DIM)
            xn = _rms(acc[:, sl], DA_HEAD_DIM) * g
            o_ref[:, sl] = (xn * c + pltpu.roll(xn, 64, 1) * s).astype(BF16)

    @pl.when((j >= n_qk) & (j < n_qk + n_v))
    def _():
        o_ref[...] = acc.astype(BF16)

    @pl.when(j >= n_qk + n_v)
    def _():
        o_ref[...] = _silu(acc).astype(BF16)


def _silu_epilogue(acc, j, extra, outs):
    outs[0][...] = _silu(acc).astype(BF16)


def _f32_epilogue(acc, j, extra, outs):
    outs[0][...] = acc


def _mla_rope(zn, c_m, s_m1, s_m2):
    return zn * c_m + pltpu.roll(zn, 96, 1) * s_m1 + pltpu.roll(zn, 32, 1) * s_m2


def _mla_q_epilogue(acc, j, extra, outs):
    gn_ref, gr_ref, c_ref, s1_ref, s2_ref = extra
    (o_ref,) = outs
    for t in range(acc.shape[1] // 256):
        c0 = t * 256
        nope = slice(c0, c0 + MLA_NOPE)
        pe = slice(c0 + MLA_NOPE, c0 + 256)
        o_ref[:, nope] = (_rms(acc[:, nope], MLA_NOPE) * gn_ref[...]).astype(BF16)
        zn = _rms(acc[:, pe], MLA_ROPE) * gr_ref[...]
        o_ref[:, pe] = _mla_rope(zn, c_ref[...], s1_ref[...], s2_ref[...]).astype(BF16)


def _mla_kv_epilogue(acc, j, extra, outs, kpe_scr):
    gn_ref, gr_ref, kpe_ref, c_ref, s1_ref, s2_ref = extra
    k_ref, v_ref = outs

    @pl.when(j == 0)
    def _():
        zn = _rms(kpe_ref[...], MLA_ROPE) * gr_ref[...]
        kpe_scr[...] = _mla_rope(zn, c_ref[...], s1_ref[...], s2_ref[...]).astype(BF16)

    for t in range(acc.shape[1] // 256):
        c0 = t * 256
        nope = slice(c0, c0 + MLA_NOPE)
        k_ref[:, nope] = (_rms(acc[:, nope], MLA_NOPE) * gn_ref[...]).astype(BF16)
        k_ref[:, c0 + MLA_NOPE:c0 + 256] = kpe_scr[...]
        v_ref[:, t * MLA_V:(t + 1) * MLA_V] = acc[:, c0 + MLA_NOPE:c0 + 256].astype(BF16)


def _residual_epilogue(acc, j, extra, outs):
    outs[0][...] = extra[0][...] + acc


def _ple_epilogue(acc, j, extra, outs):
    h_ref, p_ref, wp_ref = extra
    proj = jnp.dot(p_ref[...].astype(BF16), wp_ref[...], preferred_element_type=F32)
    outs[0][...] = h_ref[...] + jax.nn.sigmoid(acc) * proj


def _chunk_mask(bq, bk):
    r = lax.broadcasted_iota(jnp.int32, (bq, bk), 0) // CHUNK
    c = lax.broadcasted_iota(jnp.int32, (bq, bk), 1) // CHUNK
    return c <= r


def _lane_fold(x, op):
    out = x[:, :LANES]
    for t in range(1, x.shape[1] // LANES):
        out = op(out, x[:, t * LANES:(t + 1) * LANES])
    return out


def _pair_loop(n, fn):
    def body(t, carry):
        fn(2 * t)
        fn(2 * t + 1)
        return carry

    lax.fori_loop(0, lax.shift_right_logical(n, 1), body, 0)

    @pl.when((n & 1) == 1)
    def _():
        fn(n - 1)


def _scores_phase(qs, k_ref, kcols, s_ref, m_ref, i, bq):
    ncomp = len(qs)

    def block(j, mask):
        rows = pl.ds(pl.multiple_of(j * bq, bq), bq)
        folded = []
        for c in range(ncomp):
            s = lax.dot_general(qs[c], k_ref[0, rows, kcols[c]], (((1,), (1,)), ((), ())),
                                preferred_element_type=F32)
            if mask is not None:
                s = jnp.where(mask, s, -jnp.inf)
            s_ref[c, j] = s
            folded.append(_lane_fold(s, jnp.maximum))
        return folded

    for c, f in enumerate(block(i, _chunk_mask(bq, bq))):
        m_ref[c] = f

    def update(j):
        for c, f in enumerate(block(j, None)):
            m_ref[c] = jnp.maximum(m_ref[c], f)

    _pair_loop(i, update)
    for c in range(ncomp):
        m_ref[c] = jnp.broadcast_to(jnp.max(m_ref[c], axis=-1, keepdims=True), (bq, LANES))


def _values_phase(v_ref, s_ref, m_ref, l_ref, acc_ref, ncomp, i, bq):
    def block(j, first):
        v = v_ref[0, pl.ds(pl.multiple_of(j * bq, bq), bq), :]
        for c in range(ncomp):
            m = jnp.concatenate([m_ref[c]] * (bq // LANES), axis=1)
            p = jnp.exp2(s_ref[c, j] - m)
            pv = jnp.dot(p.astype(BF16), v, preferred_element_type=F32)
            lp = _lane_fold(p, jnp.add)
            if first:
                l_ref[c] = lp
                acc_ref[c] = pv
            else:
                l_ref[c] += lp
                acc_ref[c] += pv

    block(i, True)
    _pair_loop(i, lambda j: block(j, False))


def _da_attn_kernel(q_ref, k_ref, v_ref, gate_ref, lq1_ref, lk1_ref, lq2_ref, lk2_ref, sg_ref,
                    o_ref, s_ref, m_ref, l_ref, acc_ref, *, bq, lam_init):
    i = pl.program_id(2)
    d = DA_HEAD_DIM
    q = q_ref[0]
    cols = [slice(0, d), slice(d, 2 * d)]
    _scores_phase([q[:, c] for c in cols], k_ref, cols, s_ref, m_ref, i, bq)
    _values_phase(v_ref, s_ref, m_ref, l_ref, acc_ref, 2, i, bq)
    lam = (jnp.exp(jnp.sum(lq1_ref[...] * lk1_ref[...], axis=-1, keepdims=True))
           - jnp.exp(jnp.sum(lq2_ref[...] * lk2_ref[...], axis=-1, keepdims=True)) + lam_init)
    l1 = jnp.sum(l_ref[0], axis=-1, keepdims=True)
    l2 = jnp.sum(l_ref[1], axis=-1, keepdims=True)
    o = acc_ref[0] / l1 - lam * (acc_ref[1] / l2)
    o = _rms(o, DA_V_DIM) * sg_ref[...] * (1.0 - lam_init)
    o_ref[0] = (o * gate_ref[0].astype(F32)).astype(BF16)


def _attn_scratch(ncomp, S, bq, dv):
    return [pltpu.VMEM((ncomp, S // bq, bq, bq), F32), pltpu.VMEM((ncomp, bq, LANES), F32),
            pltpu.VMEM((ncomp, bq, LANES), F32), pltpu.VMEM((ncomp, bq, dv), F32)]


def _da_attention(u3, lq1, lk1, lq2, lk2, sub_gain, lam_init):
    B, S, _ = u3.shape
    bq = min(ATTN_BLOCK, S)
    H = DA_HEADS
    vec = lambda n: pl.BlockSpec((1, n), lambda b, h, i: (0, 0))
    kern = functools.partial(_da_attn_kernel, bq=bq, lam_init=lam_init)
    return pl.pallas_call(
        kern,
        out_shape=jax.ShapeDtypeStruct((B, S, H * DA_V_DIM), BF16),
        grid=(B, H, S // bq),
        in_specs=[
            pl.BlockSpec((1, bq, 256), lambda b, h, i: (b, i, h)),
            pl.BlockSpec((1, S, 256), lambda b, h, i: (b, 0, H + h)),
            pl.BlockSpec((1, S, 256), lambda b, h, i: (b, 0, 2 * H + h)),
            pl.BlockSpec((1, bq, 256), lambda b, h, i: (b, i, 3 * H + h)),
            vec(128), vec(128), vec(128), vec(128), vec(256),
        ],
        out_specs=pl.BlockSpec((1, bq, 256), lambda b, h, i: (b, i, h)),
        scratch_shapes=_attn_scratch(2, S, bq, DA_V_DIM),
        compiler_params=_cparams("parallel", "parallel", "arbitrary"),
        name="da_attention",
    )(u3, u3, u3, u3, lq1.reshape(1, 128), lk1.reshape(1, 128), lq2.reshape(1, 128),
      lk2.reshape(1, 128), sub_gain.reshape(1, 256))


def _mla_attn_kernel(q_ref, k_ref, v_ref, gate_ref, o_ref, s_ref, m_ref, l_ref, acc_ref, *, bq):
    i = pl.program_id(2)
    _scores_phase([q_ref[0]], k_ref, [slice(None)], s_ref, m_ref, i, bq)
    _values_phase(v_ref, s_ref, m_ref, l_ref, acc_ref, 1, i, bq)
    o = acc_ref[0] / jnp.sum(l_ref[0], axis=-1, keepdims=True)
    o_ref[0] = (o * gate_ref[0].astype(F32)).astype(BF16)


def _mla_attention(q3, k3, v3, gate3):
    B, S, _ = q3.shape
    bq = min(ATTN_BLOCK, S)
    H = MLA_HEADS
    kern = functools.partial(_mla_attn_kernel, bq=bq)
    return pl.pallas_call(
        kern,
        out_shape=jax.ShapeDtypeStruct((B, S, H * MLA_V), BF16),
        grid=(B, H, S // bq),
        in_specs=[
            pl.BlockSpec((1, bq, 256), lambda b, h, i: (b, i, h)),
            pl.BlockSpec((1, S, 256), lambda b, h, i: (b, 0, h)),
            pl.BlockSpec((1, S, MLA_V), lambda b, h, i: (b, 0, h)),
            pl.BlockSpec((1, bq, MLA_V), lambda b, h, i: (b, i, h)),
        ],
        out_specs=pl.BlockSpec((1, bq, MLA_V), lambda b, h, i: (b, i, h)),
        scratch_shapes=_attn_scratch(1, S, bq, MLA_V),
        compiler_params=_cparams("parallel", "parallel", "arbitrary"),
        name="mla_attention",
    )(q3, k3, v3, gate3)


def _lru_kernel(xb_ref, gate_ref, cw_ref, cb_ref, wa_ref, ba_ref, wx_ref, bx_ref, lam_ref,
                o_ref, ext_ref, xc_ref, a_ref, b_ref, h_ref, *, ts):
    t = pl.program_id(1)
    W = xb_ref.shape[-1]

    @pl.when(t == 0)
    def _():
        ext_ref[0:8, :] = jnp.zeros((8, W), F32)
        h_ref[...] = jnp.zeros((1, W), F32)

    ext_ref[8:8 + ts, :] = xb_ref[0]
    xc_ref[...] = (cb_ref[...]
                   + cw_ref[3:4, :] * ext_ref[8:8 + ts, :]
                   + cw_ref[2:3, :] * ext_ref[7:7 + ts, :]
                   + cw_ref[1:2, :] * ext_ref[6:6 + ts, :]
                   + cw_ref[0:1, :] * ext_ref[5:5 + ts, :])
    ext_ref[0:8, :] = ext_ref[ts:ts + 8, :]

    nl = -lam_ref[...]
    softplus = jnp.maximum(nl, 0.0) + jnp.log1p(jnp.exp(-jnp.abs(nl)))
    for g in range(LRU_BLOCKS):
        sl = slice(g * LRU_BLOCK, (g + 1) * LRU_BLOCK)
        xc = xc_ref[:, sl]
        xcb = xc.astype(BF16)
        r = jax.nn.sigmoid(jnp.dot(xcb, wa_ref[g], preferred_element_type=F32) + ba_ref[:, sl])
        ig = jax.nn.sigmoid(jnp.dot(xcb, wx_ref[g], preferred_element_type=F32) + bx_ref[:, sl])
        log_a = -LRU_C * r * softplus[:, sl]
        th = jnp.tanh(log_a)
        a_ref[:, sl] = jnp.exp(log_a)
        b_ref[:, sl] = jnp.sqrt(-2.0 * th / (1.0 - th)) * (ig * xc)

    def body(r8, h):
        base = pl.multiple_of(r8 * 8, 8)
        for r in range(8):
            h = a_ref[pl.ds(base + r, 1), :] * h + b_ref[pl.ds(base + r, 1), :]
            b_ref[pl.ds(base + r, 1), :] = h
        return h

    h_ref[...] = lax.fori_loop(0, ts // 8, body, h_ref[...])
    o_ref[0] = (b_ref[...] * gate_ref[0].astype(F32)).astype(BF16)


def _lru_mixer(xb3, gate3, conv_w, conv_b, w_a, b_a, w_x, b_x, lam):
    B, S, W = xb3.shape
    ts = min(LRU_TIME_TILE, S)
    vec = pl.BlockSpec((1, W), lambda b, t: (0, 0))
    wblk = pl.BlockSpec((LRU_BLOCKS, LRU_BLOCK, LRU_BLOCK), lambda b, t: (0, 0, 0))
    tile = pl.BlockSpec((1, ts, W), lambda b, t: (b, t, 0))
    kern = functools.partial(_lru_kernel, ts=ts)
    return pl.pallas_call(
        kern,
        out_shape=jax.ShapeDtypeStruct((B, S, W), BF16),
        grid=(B, S // ts),
        in_specs=[tile, tile, pl.BlockSpec((CONV_WIDTH, W), lambda b, t: (0, 0)), vec,
                  wblk, vec, wblk, vec, vec],
        out_specs=tile,
        scratch_shapes=[pltpu.VMEM((ts + 8, W), F32), pltpu.VMEM((ts, W), F32),
                        pltpu.VMEM((ts, W), F32), pltpu.VMEM((ts, W), F32),
                        pltpu.VMEM((1, W), F32)],
        compiler_params=_cparams("parallel", "arbitrary"),
        name="rglru",
    )(xb3, gate3, conv_w, conv_b.reshape(1, W), w_a.astype(BF16), b_a.reshape(1, W),
      w_x.astype(BF16), b_x.reshape(1, W), lam.reshape(1, W))


def _const_spec(shape):
    return pl.BlockSpec(shape, lambda i, j: (0,) * len(shape))


def _diff_layer(h, B, S, tabs, gain, w_in, q_gain, k_gain, lq1, lk1, lq2, lk2, sub_gain, layer_idx):
    T = h.shape[0]
    tm = min(ROW_TILE, T)
    c_da, s_da = tabs[0], tabs[1]
    scale = DA_HEAD_DIM ** -0.5 * LOG2E
    qk_gain = jnp.stack([q_gain * scale, k_gain]).reshape(2, 1, DA_HEAD_DIM).astype(F32)
    n_q = DA_HEADS * 2 * DA_HEAD_DIM // COL_TILE
    tab_spec = pl.BlockSpec((tm, LANES), lambda i, j: (i, 0))
    u = _projection(
        h, w_in.astype(BF16), gain=gain, epilogue=_da_in_epilogue,
        extras=[(qk_gain, pl.BlockSpec((1, 1, DA_HEAD_DIM), lambda i, j: (jnp.minimum(j // n_q, 1), 0, 0))),
                (c_da, tab_spec), (s_da, tab_spec)],
        out_shapes=[jax.ShapeDtypeStruct((T, w_in.shape[1]), BF16)],
        out_specs=[pl.BlockSpec((tm, COL_TILE), lambda i, j: (i, j))],
        name="da_in_proj")[0]
    lam_init = 0.8 - 0.6 * math.exp(-0.3 * layer_idx)
    y = _da_attention(u.reshape(B, S, -1), lq1, lk1, lq2, lk2, sub_gain, lam_init)
    return y.reshape(T, -1)


def _mla_layer(h, B, S, tabs, gain, w_in, cq_gain, ckv_gain, w_uq, w_ukv, qn_gain, qr_gain,
               kn_gain, kr_gain):
    T = h.shape[0]
    tm = min(ROW_TILE, T)
    c_m, s_m1, s_m2 = tabs[2], tabs[3], tabs[4]
    H = MLA_HEADS
    n_lat = MLA_Q_RANK + MLA_KV_RANK + MLA_ROPE
    lat_w = n_lat + (LANES - MLA_ROPE)
    w_lat = jnp.pad(w_in[:, :n_lat], ((0, 0), (0, lat_w - n_lat))).astype(BF16)
    w_gate = w_in[:, n_lat:].astype(BF16)
    lat = _projection(
        h, w_lat, gain=gain, epilogue=_f32_epilogue, tn=lat_w,
        out_shapes=[jax.ShapeDtypeStruct((T, lat_w), F32)],
        out_specs=[pl.BlockSpec((tm, lat_w), lambda i, j: (i, j))],
        name="mla_latent_proj")[0]
    gate = _projection(
        h, w_gate, gain=gain, epilogue=_silu_epilogue,
        out_shapes=[jax.ShapeDtypeStruct((T, w_gate.shape[1]), BF16)],
        out_specs=[pl.BlockSpec((tm, COL_TILE), lambda i, j: (i, j))],
        name="mla_gate_proj")[0]

    scale = (MLA_NOPE + MLA_ROPE) ** -0.5 * LOG2E
    pad_r = lambda g: jnp.pad(g, (0, LANES - MLA_ROPE)).reshape(1, LANES).astype(F32)
    hd = MLA_NOPE + MLA_ROPE
    w_uq_p = jnp.pad(w_uq.reshape(MLA_Q_RANK, H, hd), ((0, 0), (0, 0), (0, 256 - hd)))
    w_uq_p = w_uq_p.reshape(MLA_Q_RANK, H * 256).astype(BF16)
    vec = _const_spec((1, LANES))
    tab_spec = pl.BlockSpec((tm, LANES), lambda i, j: (i, 0))
    tab_extras = [(c_m, tab_spec), (s_m1, tab_spec), (s_m2, tab_spec)]
    q = _projection(
        lat, w_uq_p, gain=cq_gain, x_col_block=0, epilogue=_mla_q_epilogue, tn=MLA_HEAD_TILE,
        extras=[((qn_gain * scale).reshape(1, LANES), vec), (pad_r(qr_gain * scale), vec)] + tab_extras,
        out_shapes=[jax.ShapeDtypeStruct((T, H * 256), BF16)],
        out_specs=[pl.BlockSpec((tm, MLA_HEAD_TILE), lambda i, j: (i, j))],
        name="mla_q_proj")[0]
    kpe_block = (MLA_Q_RANK + MLA_KV_RANK) // LANES
    k, v = _projection(
        lat, w_ukv.astype(BF16), gain=ckv_gain, x_col_block=1, epilogue=_mla_kv_epilogue,
        tn=MLA_HEAD_TILE, scratch=[pltpu.VMEM((tm, LANES), BF16)],
        extras=[(kn_gain.reshape(1, LANES), vec), (pad_r(kr_gain), vec),
                (lat, pl.BlockSpec((tm, LANES), lambda i, j: (i, kpe_block)))] + tab_extras,
        out_shapes=[jax.ShapeDtypeStruct((T, H * 256), BF16), jax.ShapeDtypeStruct((T, H * MLA_V), BF16)],
        out_specs=[pl.BlockSpec((tm, MLA_HEAD_TILE), lambda i, j: (i, j)),
                   pl.BlockSpec((tm, MLA_HEAD_TILE // 2), lambda i, j: (i, j))],
        name="mla_kv_proj")
    y = _mla_attention(q.reshape(B, S, -1), k.reshape(B, S, -1), v.reshape(B, S, -1),
                       gate.reshape(B, S, -1))
    return y.reshape(T, -1)


def _lru_layer(h, B, S, gain, w_in, conv_w, conv_b, w_a, b_a, w_x, b_x, lam):
    T = h.shape[0]
    tm = min(ROW_TILE, T)
    W = D_MODEL
    xb = _projection(
        h, w_in[:, :W].astype(BF16), gain=gain, epilogue=_f32_epilogue,
        out_shapes=[jax.ShapeDtypeStruct((T, W), F32)],
        out_specs=[pl.BlockSpec((tm, COL_TILE), lambda i, j: (i, j))],
        name="lru_x_proj")[0]
    gate = _projection(
        h, w_in[:, W:].astype(BF16), gain=gain, epilogue=_silu_epilogue,
        out_shapes=[jax.ShapeDtypeStruct((T, W), BF16)],
        out_specs=[pl.BlockSpec((tm, COL_TILE), lambda i, j: (i, j))],
        name="lru_gate_proj")[0]
    y = _lru_mixer(xb.reshape(B, S, W), gate.reshape(B, S, W), conv_w, conv_b, w_a, b_a, w_x, b_x, lam)
    return y.reshape(T, W)


def _post_mixer(h, y, w_out, p_i, ple_gain, w_gate, w_proj):
    T = h.shape[0]
    tm = min(ROW_TILE, T)
    tile = pl.BlockSpec((tm, COL_TILE), lambda i, j: (i, j))
    out = [jax.ShapeDtypeStruct((T, D_MODEL), F32)]
    h = _projection(y, w_out.astype(BF16), epilogue=_residual_epilogue, extras=[(h, tile)],
                    out_shapes=out, out_specs=[tile], name="out_proj")[0]
    P = p_i.shape[1]
    tile = pl.BlockSpec((tm, PLE_COL_TILE), lambda i, j: (i, j))
    return _projection(
        h, w_gate.astype(BF16), gain=ple_gain, epilogue=_ple_epilogue, tn=PLE_COL_TILE,
        extras=[(h, tile), (p_i, pl.BlockSpec((tm, P), lambda i, j: (i, 0))),
                (w_proj.astype(BF16), pl.BlockSpec((P, PLE_COL_TILE), lambda i, j: (0, j)))],
        out_shapes=out, out_specs=[tile], name="ple")[0]


def kernel(x, p, positions, norm_gain, a_w_in, a_q_norm, a_k_norm, a_lambda_q1, a_lambda_k1, a_lambda_q2, a_lambda_k2, a_sub_norm, a_w_out, b_w_in, b_cq_norm, b_ckv_norm, b_w_uq, b_w_ukv, b_q_nope_norm, b_q_rope_norm, b_k_nope_norm, b_k_rope_norm, b_w_out, c_w_in, c_conv_w, c_conv_b, c_w_a, c_b_a, c_w_x, c_b_x, c_lambda, c_w_out, ple_norm, ple_w_gate, ple_w_proj):
    B, S, D = x.shape
    T = B * S
    depth = p.shape[0]
    h = x.reshape(T, D)
    tabs = _rope_tables(positions.reshape(T, 1))
    for i in range(depth):
        j = i // N_MIXERS
        kind = i % N_MIXERS
        if kind == 0:
            y = _diff_layer(h, B, S, tabs, norm_gain[i], a_w_in[j], a_q_norm[j], a_k_norm[j],
                            a_lambda_q1[j], a_lambda_k1[j], a_lambda_q2[j], a_lambda_k2[j],
                            a_sub_norm[j], i)
            w_out = a_w_out[j]
        elif kind == 1:
            y = _mla_layer(h, B, S, tabs, norm_gain[i], b_w_in[j], b_cq_norm[j], b_ckv_norm[j],
                           b_w_uq[j], b_w_ukv[j], b_q_nope_norm[j], b_q_rope_norm[j],
                           b_k_nope_norm[j], b_k_rope_norm[j])
            w_out = b_w_out[j]
        else:
            y = _lru_layer(h, B, S, norm_gain[i], c_w_in[j], c_conv_w[j], c_conv_b[j], c_w_a[j],
                           c_b_a[j], c_w_x[j], c_b_x[j], c_lambda[j])
            w_out = c_w_out[j]
        h = _post_mixer(h, y, w_out, p[i].reshape(T, -1), ple_norm[i], ple_w_gate[i], ple_w_proj[i])
    return h.reshape(B, S, D)
```

```python
import functools
import math

import jax
import jax.numpy as jnp
from jax import lax
from jax.experimental import pallas as pl
from jax.experimental.pallas import tpu as pltpu

F32 = jnp.float32
BF16 = jnp.bfloat16

D_MODEL = 2048
CHUNK = 64
ROPE_THETA = 10000.0
EPS = 1e-6
N_MIXERS = 3

DA_HEAD_DIM = 128
DA_V_DIM = 256
DA_HEADS = 8
MLA_NOPE = 128
MLA_ROPE = 64
MLA_V = 128
MLA_HEADS = 16
MLA_Q_RANK = 512
MLA_KV_RANK = 512
LRU_BLOCKS = 8
LRU_BLOCK = 256
CONV_WIDTH = 4
LRU_C = 8.0

LANES = 128
VMEM_LIMIT = 56 * 1024 * 1024
ROW_TILE = 1024
COL_TILE = 1024
PLE_ROW_TILE = 512
PLE_COL_TILE = 512
MLA_HEAD_TILE = 512
LOG2E = math.log2(math.e)
ATTN_BLOCK = 512
LRU_TIME_TILE = 256


def _cparams(*sem):
    return pltpu.CompilerParams(dimension_semantics=sem, vmem_limit_bytes=VMEM_LIMIT)


def _rope_table_kernel(pos_ref, inv_da_ref, inv_mla_ref, c_da, s_da, c_m, s_m1, s_m2):
    pos = pos_ref[...].astype(F32)
    lane = lax.broadcasted_iota(jnp.int32, c_da.shape, 1)
    ang = pos * inv_da_ref[...]
    sn = jnp.sin(ang)
    c_da[...] = jnp.cos(ang)
    s_da[...] = jnp.where(lane < 64, -sn, sn)
    ang = pos * inv_mla_ref[...]
    cs = jnp.cos(ang)
    sn = jnp.sin(ang)
    c_m[...] = jnp.where(lane < 64, cs, 0.0)
    s_m1[...] = jnp.where(lane < 32, -sn, 0.0)
    s_m2[...] = jnp.where((lane >= 32) & (lane < 64), sn, 0.0)


def _rope_tables(pos_col):
    T = pos_col.shape[0]
    tr = min(2048, T)
    half = DA_HEAD_DIM // 2
    inv = ROPE_THETA ** (-jnp.arange(half, dtype=F32) / half)
    inv_da = jnp.concatenate([inv, inv])[None, :]
    half = MLA_ROPE // 2
    inv = ROPE_THETA ** (-jnp.arange(half, dtype=F32) / half)
    inv_mla = jnp.concatenate([inv, inv, inv, inv])[None, :]
    tab = jax.ShapeDtypeStruct((T, LANES), F32)
    row = pl.BlockSpec((tr, LANES), lambda i: (i, 0))
    vec = pl.BlockSpec((1, LANES), lambda i: (0, 0))
    return pl.pallas_call(
        _rope_table_kernel,
        out_shape=[tab] * 5,
        grid=(T // tr,),
        in_specs=[pl.BlockSpec((tr, 1), lambda i: (i, 0)), vec, vec],
        out_specs=[row] * 5,
        compiler_params=_cparams("parallel"),
        name="rope_tables",
    )(pos_col, inv_da, inv_mla)


def _proj_kernel(x_ref, g_ref, w_ref, *rest, n_extra, n_out, norm, epilogue):
    extra = rest[:n_extra]
    outs = rest[n_extra:n_extra + n_out]
    scratch = rest[n_extra + n_out:]
    j = pl.program_id(1)
    if norm:
        xn_ref, scratch = scratch[0], scratch[1:]

        @pl.when(j == 0)
        def _():
            x = x_ref[...].astype(F32)
            ms = jnp.mean(x * x, axis=-1, keepdims=True)
            xn_ref[...] = (x * lax.rsqrt(ms + EPS) * g_ref[...]).astype(BF16)

        lhs = xn_ref[...]
    else:
        lhs = x_ref[...]
    acc = jnp.dot(lhs, w_ref[...], preferred_element_type=F32)
    epilogue(acc, j, extra, outs, *scratch)


def _projection(x, w, *, epilogue, out_shapes, out_specs, gain=None, x_col_block=0,
                extras=(), scratch=(), tm=ROW_TILE, tn=COL_TILE, name):
    T = x.shape[0]
    K, N = w.shape
    tm = min(tm, T)
    tn = min(tn, N)
    norm = gain is not None
    if not norm:
        gain = jnp.zeros((1, K), F32)
    in_specs = [
        pl.BlockSpec((tm, K), lambda i, j: (i, x_col_block)),
        pl.BlockSpec((1, K), lambda i, j: (0, 0)),
        pl.BlockSpec((K, tn), lambda i, j: (0, j)),
    ] + [spec for _, spec in extras]
    kern = functools.partial(_proj_kernel, n_extra=len(extras), n_out=len(out_shapes),
                             norm=norm, epilogue=epilogue)
    return pl.pallas_call(
        kern,
        out_shape=out_shapes,
        grid=(T // tm, N // tn),
        in_specs=in_specs,
        out_specs=out_specs,
        scratch_shapes=([pltpu.VMEM((tm, K), BF16)] if norm else []) + list(scratch),
        compiler_params=_cparams("parallel", "arbitrary"),
        name=name,
    )(x, gain.reshape(1, K).astype(F32), w, *[a for a, _ in extras])


def _rms(x, width):
    return x * lax.rsqrt(jnp.sum(x * x, axis=-1, keepdims=True) * (1.0 / width) + EPS)


def _sigmoid(x):
    return 0.5 * jnp.tanh(0.5 * x) + 0.5


def _silu(x):
    return x * jax.nn.sigmoid(x)


def _da_in_epilogue(acc, j, extra, outs):
    g_ref, c_ref, s_ref = extra
    (o_ref,) = outs
    n_qk = 2 * DA_HEADS * 2 * DA_HEAD_DIM // COL_TILE
    n_v = DA_HEADS * DA_V_DIM // COL_TILE

    @pl.when(j < n_qk)
    def _():
        g = g_ref[0]
        c = c_ref[...]
        s = s_ref[...]
        for t in range(COL_TILE // DA_HEAD_DIM):
            sl = slice(t * DA_HEAD_DIM, (t + 1) * DA_HEAD_DIM)
            xn = _rms(acc[:, sl], DA_HEAD_DIM) * g
            o_ref[:, sl] = (xn * c + pltpu.roll(xn, 64, 1) * s).astype(BF16)

    @pl.when((j >= n_qk) & (j < n_qk + n_v))
    def _():
        o_ref[...] = acc.astype(BF16)

    @pl.when(j >= n_qk + n_v)
    def _():
        o_ref[...] = _silu(acc).astype(BF16)


def _silu_epilogue(acc, j, extra, outs):
    outs[0][...] = _silu(acc).astype(BF16)


def _f32_epilogue(acc, j, extra, outs):
    outs[0][...] = acc


def _mla_rope(zn, c_m, s_m1, s_m2):
    return zn * c_m + pltpu.roll(zn, 96, 1) * s_m1 + pltpu.roll(zn, 32, 1) * s_m2


def _mla_q_epilogue(acc, j, extra, outs):
    gn_ref, gr_ref, c_ref, s1_ref, s2_ref = extra
    (o_ref,) = outs
    for t in range(acc.shape[1] // 256):
        c0 = t * 256
        nope = slice(c0, c0 + MLA_NOPE)
        pe = slice(c0 + MLA_NOPE, c0 + 256)
        o_ref[:, nope] = (_rms(acc[:, nope], MLA_NOPE) * gn_ref[...]).astype(BF16)
        zn = _rms(acc[:, pe], MLA_ROPE) * gr_ref[...]
        o_ref[:, pe] = _mla_rope(zn, c_ref[...], s1_ref[...], s2_ref[...]).astype(BF16)


def _mla_kv_epilogue(acc, j, extra, outs, kpe_scr):
    gn_ref, gr_ref, kpe_ref, c_ref, s1_ref, s2_ref = extra
    k_ref, v_ref = outs

    @pl.when(j == 0)
    def _():
        zn = _rms(kpe_ref[...], MLA_ROPE) * gr_ref[...]
        kpe_scr[...] = _mla_rope(zn, c_ref[...], s1_ref[...], s2_ref[...]).astype(BF16)

    for t in range(acc.shape[1] // 256):
        c0 = t * 256
        nope = slice(c0, c0 + MLA_NOPE)
        k_ref[:, nope] = (_rms(acc[:, nope], MLA_NOPE) * gn_ref[...]).astype(BF16)
        k_ref[:, c0 + MLA_NOPE:c0 + 256] = kpe_scr[...]
        v_ref[:, t * MLA_V:(t + 1) * MLA_V] = acc[:, c0 + MLA_NOPE:c0 + 256].astype(BF16)


def _residual_epilogue(acc, j, extra, outs):
    outs[0][...] = extra[0][...] + acc


def _ple_kernel(x_ref, g_ref, wg_ref, p_ref, wp_ref, o_ref):
    xn = (_rms(x_ref[...], x_ref.shape[1]) * g_ref[...]).astype(BF16)
    pb = p_ref[...].astype(BF16)
    for c0 in range(0, o_ref.shape[1], PLE_COL_TILE):
        cols = slice(c0, c0 + PLE_COL_TILE)
        gate = jnp.dot(xn, wg_ref[:, cols], preferred_element_type=F32)
        proj = jnp.dot(pb, wp_ref[:, cols], preferred_element_type=F32)
        o_ref[:, cols] = x_ref[:, cols] + jax.nn.sigmoid(gate) * proj


def _ple(h, p_i, gain, w_gate, w_proj):
    T, D = h.shape
    P = p_i.shape[1]
    tm = min(PLE_ROW_TILE, T)
    row = lambda w: pl.BlockSpec((tm, w), lambda i: (i, 0))
    const = lambda r, c: pl.BlockSpec((r, c), lambda i: (0, 0))
    return pl.pallas_call(
        _ple_kernel,
        out_shape=jax.ShapeDtypeStruct((T, D), F32),
        grid=(T // tm,),
        in_specs=[row(D), const(1, D), const(D, D), row(P), const(P, D)],
        out_specs=row(D),
        compiler_params=_cparams("parallel"),
        name="ple",
    )(h, gain.reshape(1, D).astype(F32), w_gate.astype(BF16), p_i, w_proj.astype(BF16))


def _chunk_mask(bq, bk):
    r = lax.broadcasted_iota(jnp.int32, (bq, bk), 0) // CHUNK
    c = lax.broadcasted_iota(jnp.int32, (bq, bk), 1) // CHUNK
    return c <= r


def _lane_fold(x, op):
    out = x[:, :LANES]
    for t in range(1, x.shape[1] // LANES):
        out = op(out, x[:, t * LANES:(t + 1) * LANES])
    return out


def _block_loop(n, fn):
    def body(t, carry):
        for u in range(4):
            fn(4 * t + u)
        return carry

    quads = lax.shift_right_logical(n, 2)
    lax.fori_loop(0, quads, body, 0)

    @pl.when((n & 2) == 2)
    def _():
        fn(4 * quads)
        fn(4 * quads + 1)

    @pl.when((n & 1) == 1)
    def _():
        fn(n - 1)


def _scores_phase(qs, k_ref, kcols, s_ref, m_ref, i, bq):
    ncomp = len(qs)

    def block(j, mask):
        rows = pl.ds(pl.multiple_of(j * bq, bq), bq)
        folded = []
        for c in range(ncomp):
            s = lax.dot_general(qs[c], k_ref[0, rows, kcols[c]], (((1,), (1,)), ((), ())),
                                preferred_element_type=F32)
            if mask is not None:
                s = jnp.where(mask, s, -jnp.inf)
            s_ref[c, j] = s
            folded.append(_lane_fold(s, jnp.maximum))
        return folded

    for c, f in enumerate(block(i, _chunk_mask(bq, bq))):
        m_ref[c] = f

    def update(j):
        for c, f in enumerate(block(j, None)):
            m_ref[c] = jnp.maximum(m_ref[c], f)

    _block_loop(i, update)
    for c in range(ncomp):
        m_ref[c] = jnp.broadcast_to(jnp.max(m_ref[c], axis=-1, keepdims=True), (bq, LANES))


def _values_phase(v_ref, s_ref, m_ref, l_ref, acc_ref, ncomp, i, bq):
    def block(j, first):
        v = v_ref[0, pl.ds(pl.multiple_of(j * bq, bq), bq), :]
        for c in range(ncomp):
            m = jnp.concatenate([m_ref[c]] * (bq // LANES), axis=1)
            p = jnp.exp2(s_ref[c, j] - m)
            pv = jnp.dot(p.astype(BF16), v, preferred_element_type=F32)
            lp = _lane_fold(p, jnp.add)
            if first:
                l_ref[c] = lp
                acc_ref[c] = pv
            else:
                l_ref[c] += lp
                acc_ref[c] += pv

    block(i, True)
    _block_loop(i, lambda j: block(j, False))


def _da_attn_kernel(q_ref, k_ref, v_ref, gate_ref, lq1_ref, lk1_ref, lq2_ref, lk2_ref, sg_ref,
                    o_ref, s_ref, m_ref, l_ref, acc_ref, *, bq, lam_init):
    i = pl.program_id(2)
    d = DA_HEAD_DIM
    q = q_ref[0]
    cols = [slice(0, d), slice(d, 2 * d)]
    _scores_phase([q[:, c] for c in cols], k_ref, cols, s_ref, m_ref, i, bq)
    _values_phase(v_ref, s_ref, m_ref, l_ref, acc_ref, 2, i, bq)
    lam = (jnp.exp(jnp.sum(lq1_ref[...] * lk1_ref[...], axis=-1, keepdims=True))
           - jnp.exp(jnp.sum(lq2_ref[...] * lk2_ref[...], axis=-1, keepdims=True)) + lam_init)
    l1 = jnp.sum(l_ref[0], axis=-1, keepdims=True)
    l2 = jnp.sum(l_ref[1], axis=-1, keepdims=True)
    o = acc_ref[0] / l1 - lam * (acc_ref[1] / l2)
    o = _rms(o, DA_V_DIM) * sg_ref[...] * (1.0 - lam_init)
    o_ref[0] = (o * gate_ref[0].astype(F32)).astype(BF16)


def _attn_scratch(ncomp, S, bq, dv):
    return [pltpu.VMEM((ncomp, S // bq, bq, bq), F32), pltpu.VMEM((ncomp, bq, LANES), F32),
            pltpu.VMEM((ncomp, bq, LANES), F32), pltpu.VMEM((ncomp, bq, dv), F32)]


def _da_attention(u3, lq1, lk1, lq2, lk2, sub_gain, lam_init):
    B, S, _ = u3.shape
    bq = min(ATTN_BLOCK, S)
    H = DA_HEADS
    vec = lambda n: pl.BlockSpec((1, n), lambda b, h, i: (0, 0))
    kern = functools.partial(_da_attn_kernel, bq=bq, lam_init=lam_init)
    return pl.pallas_call(
        kern,
        out_shape=jax.ShapeDtypeStruct((B, S, H * DA_V_DIM), BF16),
        grid=(B, H, S // bq),
        in_specs=[
            pl.BlockSpec((1, bq, 256), lambda b, h, i: (b, i, h)),
            pl.BlockSpec((1, S, 256), lambda b, h, i: (b, 0, H + h)),
            pl.BlockSpec((1, S, 256), lambda b, h, i: (b, 0, 2 * H + h)),
            pl.BlockSpec((1, bq, 256), lambda b, h, i: (b, i, 3 * H + h)),
            vec(128), vec(128), vec(128), vec(128), vec(256),
        ],
        out_specs=pl.BlockSpec((1, bq, 256), lambda b, h, i: (b, i, h)),
        scratch_shapes=_attn_scratch(2, S, bq, DA_V_DIM),
        compiler_params=_cparams("parallel", "parallel", "arbitrary"),
        name="da_attention",
    )(u3, u3, u3, u3, lq1.reshape(1, 128), lk1.reshape(1, 128), lq2.reshape(1, 128),
      lk2.reshape(1, 128), sub_gain.reshape(1, 256))


def _mla_attn_kernel(q_ref, k_ref, v_ref, gate_ref, o_ref, s_ref, m_ref, l_ref, acc_ref, *, bq):
    i = pl.program_id(2)
    _scores_phase([q_ref[0]], k_ref, [slice(None)], s_ref, m_ref, i, bq)
    _values_phase(v_ref, s_ref, m_ref, l_ref, acc_ref, 1, i, bq)
    o = acc_ref[0] / jnp.sum(l_ref[0], axis=-1, keepdims=True)
    o_ref[0] = (o * gate_ref[0].astype(F32)).astype(BF16)


def _mla_attention(q3, k3, v3, gate3):
    B, S, _ = q3.shape
    bq = min(ATTN_BLOCK, S)
    H = MLA_HEADS
    kern = functools.partial(_mla_attn_kernel, bq=bq)
    return pl.pallas_call(
        kern,
        out_shape=jax.ShapeDtypeStruct((B, S, H * MLA_V), BF16),
        grid=(B, H, S // bq),
        in_specs=[
            pl.BlockSpec((1, bq, 256), lambda b, h, i: (b, i, h)),
            pl.BlockSpec((1, S, 256), lambda b, h, i: (b, 0, h)),
            pl.BlockSpec((1, S, MLA_V), lambda b, h, i: (b, 0, h)),
            pl.BlockSpec((1, bq, MLA_V), lambda b, h, i: (b, i, h)),
        ],
        out_specs=pl.BlockSpec((1, bq, MLA_V), lambda b, h, i: (b, i, h)),
        scratch_shapes=_attn_scratch(1, S, bq, MLA_V),
        compiler_params=_cparams("parallel", "parallel", "arbitrary"),
        name="mla_attention",
    )(q3, k3, v3, gate3)


def _lru_kernel(xb_ref, gate_ref, cw_ref, cb_ref, wa_ref, ba_ref, wx_ref, bx_ref, lam_ref,
                o_ref, ext_ref, xc_ref, a_ref, b_ref, h_ref, *, ts):
    t = pl.program_id(1)
    W = xb_ref.shape[-1]

    @pl.when(t == 0)
    def _():
        ext_ref[0:8, :] = jnp.zeros((8, W), F32)
        h_ref[...] = jnp.zeros((1, W), F32)

    ext_ref[8:8 + ts, :] = xb_ref[0]
    xc_ref[...] = (cb_ref[...]
                   + cw_ref[3:4, :] * ext_ref[8:8 + ts, :]
                   + cw_ref[2:3, :] * ext_ref[7:7 + ts, :]
                   + cw_ref[1:2, :] * ext_ref[6:6 + ts, :]
                   + cw_ref[0:1, :] * ext_ref[5:5 + ts, :])
    ext_ref[0:8, :] = ext_ref[ts:ts + 8, :]

    nl = -lam_ref[...]
    softplus = jnp.maximum(nl, 0.0) + jnp.log1p(jnp.exp(-jnp.abs(nl)))
    for g in range(LRU_BLOCKS):
        sl = slice(g * LRU_BLOCK, (g + 1) * LRU_BLOCK)
        xc = xc_ref[:, sl]
        xcb = xc.astype(BF16)
        r = _sigmoid(jnp.dot(xcb, wa_ref[g], preferred_element_type=F32) + ba_ref[:, sl])
        ig = _sigmoid(jnp.dot(xcb, wx_ref[g], preferred_element_type=F32) + bx_ref[:, sl])
        log_a = -LRU_C * r * softplus[:, sl]
        th = jnp.tanh(log_a)
        a_ref[:, sl] = jnp.exp(log_a)
        b_ref[:, sl] = jnp.sqrt(-2.0 * th / (1.0 - th)) * (ig * xc)

    def body(r8, h):
        base = pl.multiple_of(r8 * 8, 8)
        for r in range(8):
            h = a_ref[pl.ds(base + r, 1), :] * h + b_ref[pl.ds(base + r, 1), :]
            b_ref[pl.ds(base + r, 1), :] = h
        return h

    h_ref[...] = lax.fori_loop(0, ts // 8, body, h_ref[...])
    o_ref[0] = (b_ref[...] * gate_ref[0].astype(F32)).astype(BF16)


def _lru_mixer(xb3, gate3, conv_w, conv_b, w_a, b_a, w_x, b_x, lam):
    B, S, W = xb3.shape
    ts = min(LRU_TIME_TILE, S)
    vec = pl.BlockSpec((1, W), lambda b, t: (0, 0))
    wblk = pl.BlockSpec((LRU_BLOCKS, LRU_BLOCK, LRU_BLOCK), lambda b, t: (0, 0, 0))
    tile = pl.BlockSpec((1, ts, W), lambda b, t: (b, t, 0))
    kern = functools.partial(_lru_kernel, ts=ts)
    return pl.pallas_call(
        kern,
        out_shape=jax.ShapeDtypeStruct((B, S, W), BF16),
        grid=(B, S // ts),
        in_specs=[tile, tile, pl.BlockSpec((CONV_WIDTH, W), lambda b, t: (0, 0)), vec,
                  wblk, vec, wblk, vec, vec],
        out_specs=tile,
        scratch_shapes=[pltpu.VMEM((ts + 8, W), F32), pltpu.VMEM((ts, W), F32),
                        pltpu.VMEM((ts, W), F32), pltpu.VMEM((ts, W), F32),
                        pltpu.VMEM((1, W), F32)],
        compiler_params=_cparams("parallel", "arbitrary"),
        name="rglru",
    )(xb3, gate3, conv_w, conv_b.reshape(1, W), w_a.astype(BF16), b_a.reshape(1, W),
      w_x.astype(BF16), b_x.reshape(1, W), lam.reshape(1, W))


def _const_spec(shape):
    return pl.BlockSpec(shape, lambda i, j: (0,) * len(shape))


def _diff_layer(h, B, S, tabs, gain, w_in, q_gain, k_gain, lq1, lk1, lq2, lk2, sub_gain, layer_idx):
    T = h.shape[0]
    tm = min(ROW_TILE, T)
    c_da, s_da = tabs[0], tabs[1]
    scale = DA_HEAD_DIM ** -0.5 * LOG2E
    qk_gain = jnp.stack([q_gain * scale, k_gain]).reshape(2, 1, DA_HEAD_DIM).astype(F32)
    n_q = DA_HEADS * 2 * DA_HEAD_DIM // COL_TILE
    tab_spec = pl.BlockSpec((tm, LANES), lambda i, j: (i, 0))
    u = _projection(
        h, w_in.astype(BF16), gain=gain, epilogue=_da_in_epilogue,
        extras=[(qk_gain, pl.BlockSpec((1, 1, DA_HEAD_DIM), lambda i, j: (jnp.minimum(j // n_q, 1), 0, 0))),
                (c_da, tab_spec), (s_da, tab_spec)],
        out_shapes=[jax.ShapeDtypeStruct((T, w_in.shape[1]), BF16)],
        out_specs=[pl.BlockSpec((tm, COL_TILE), lambda i, j: (i, j))],
        name="da_in_proj")[0]
    lam_init = 0.8 - 0.6 * math.exp(-0.3 * layer_idx)
    y = _da_attention(u.reshape(B, S, -1), lq1, lk1, lq2, lk2, sub_gain, lam_init)
    return y.reshape(T, -1)


def _mla_layer(h, B, S, tabs, gain, w_in, cq_gain, ckv_gain, w_uq, w_ukv, qn_gain, qr_gain,
               kn_gain, kr_gain):
    T = h.shape[0]
    tm = min(ROW_TILE, T)
    c_m, s_m1, s_m2 = tabs[2], tabs[3], tabs[4]
    H = MLA_HEADS
    n_lat = MLA_Q_RANK + MLA_KV_RANK + MLA_ROPE
    lat_w = n_lat + (LANES - MLA_ROPE)
    w_lat = jnp.pad(w_in[:, :n_lat], ((0, 0), (0, lat_w - n_lat))).astype(BF16)
    w_gate = w_in[:, n_lat:].astype(BF16)
    lat = _projection(
        h, w_lat, gain=gain, epilogue=_f32_epilogue, tn=lat_w,
        out_shapes=[jax.ShapeDtypeStruct((T, lat_w), F32)],
        out_specs=[pl.BlockSpec((tm, lat_w), lambda i, j: (i, j))],
        name="mla_latent_proj")[0]
    gate = _projection(
        h, w_gate, gain=gain, epilogue=_silu_epilogue,
        out_shapes=[jax.ShapeDtypeStruct((T, w_gate.shape[1]), BF16)],
        out_specs=[pl.BlockSpec((tm, COL_TILE), lambda i, j: (i, j))],
        name="mla_gate_proj")[0]

    scale = (MLA_NOPE + MLA_ROPE) ** -0.5 * LOG2E
    pad_r = lambda g: jnp.pad(g, (0, LANES - MLA_ROPE)).reshape(1, LANES).astype(F32)
    hd = MLA_NOPE + MLA_ROPE
    w_uq_p = jnp.pad(w_uq.reshape(MLA_Q_RANK, H, hd), ((0, 0), (0, 0), (0, 256 - hd)))
    w_uq_p = w_uq_p.reshape(MLA_Q_RANK, H * 256).astype(BF16)
    vec = _const_spec((1, LANES))
    tab_spec = pl.BlockSpec((tm, LANES), lambda i, j: (i, 0))
    tab_extras = [(c_m, tab_spec), (s_m1, tab_spec), (s_m2, tab_spec)]
    q = _projection(
        lat, w_uq_p, gain=cq_gain, x_col_block=0, epilogue=_mla_q_epilogue, tn=MLA_HEAD_TILE,
        extras=[((qn_gain * scale).reshape(1, LANES), vec), (pad_r(qr_gain * scale), vec)] + tab_extras,
        out_shapes=[jax.ShapeDtypeStruct((T, H * 256), BF16)],
        out_specs=[pl.BlockSpec((tm, MLA_HEAD_TILE), lambda i, j: (i, j))],
        name="mla_q_proj")[0]
    kpe_block = (MLA_Q_RANK + MLA_KV_RANK) // LANES
    k, v = _projection(
        lat, w_ukv.astype(BF16), gain=ckv_gain, x_col_block=1, epilogue=_mla_kv_epilogue,
        tn=MLA_HEAD_TILE, scratch=[pltpu.VMEM((tm, LANES), BF16)],
        extras=[(kn_gain.reshape(1, LANES), vec), (pad_r(kr_gain), vec),
                (lat, pl.BlockSpec((tm, LANES), lambda i, j: (i, kpe_block)))] + tab_extras,
        out_shapes=[jax.ShapeDtypeStruct((T, H * 256), BF16), jax.ShapeDtypeStruct((T, H * MLA_V), BF16)],
        out_specs=[pl.BlockSpec((tm, MLA_HEAD_TILE), lambda i, j: (i, j)),
                   pl.BlockSpec((tm, MLA_HEAD_TILE // 2), lambda i, j: (i, j))],
        name="mla_kv_proj")
    y = _mla_attention(q.reshape(B, S, -1), k.reshape(B, S, -1), v.reshape(B, S, -1),
                       gate.reshape(B, S, -1))
    return y.reshape(T, -1)


def _lru_layer(h, B, S, gain, w_in, conv_w, conv_b, w_a, b_a, w_x, b_x, lam):
    T = h.shape[0]
    tm = min(ROW_TILE, T)
    W = D_MODEL
    xb = _projection(
        h, w_in[:, :W].astype(BF16), gain=gain, epilogue=_f32_epilogue,
        out_shapes=[jax.ShapeDtypeStruct((T, W), F32)],
        out_specs=[pl.BlockSpec((tm, COL_TILE), lambda i, j: (i, j))],
        name="lru_x_proj")[0]
    gate = _projection(
        h, w_in[:, W:].astype(BF16), gain=gain, epilogue=_silu_epilogue,
        out_shapes=[jax.ShapeDtypeStruct((T, W), BF16)],
        out_specs=[pl.BlockSpec((tm, COL_TILE), lambda i, j: (i, j))],
        name="lru_gate_proj")[0]
    y = _lru_mixer(xb.reshape(B, S, W), gate.reshape(B, S, W), conv_w, conv_b, w_a, b_a, w_x, b_x, lam)
    return y.reshape(T, W)


def _post_mixer(h, y, w_out, p_i, ple_gain, w_gate, w_proj):
    T = h.shape[0]
    tm = min(ROW_TILE, T)
    tile = pl.BlockSpec((tm, COL_TILE), lambda i, j: (i, j))
    out = [jax.ShapeDtypeStruct((T, D_MODEL), F32)]
    h = _projection(y, w_out.astype(BF16), epilogue=_residual_epilogue, extras=[(h, tile)],
                    out_shapes=out, out_specs=[tile], name="out_proj")[0]
    return _ple(h, p_i, ple_gain, w_gate, w_proj)


def kernel(x, p, positions, norm_gain, a_w_in, a_q_norm, a_k_norm, a_lambda_q1, a_lambda_k1, a_lambda_q2, a_lambda_k2, a_sub_norm, a_w_out, b_w_in, b_cq_norm, b_ckv_norm, b_w_uq, b_w_ukv, b_q_nope_norm, b_q_rope_norm, b_k_nope_norm, b_k_rope_norm, b_w_out, c_w_in, c_conv_w, c_conv_b, c_w_a, c_b_a, c_w_x, c_b_x, c_lambda, c_w_out, ple_norm, ple_w_gate, ple_w_proj):
    B, S, D = x.shape
    T = B * S
    depth = p.shape[0]
    h = x.reshape(T, D)
    tabs = _rope_tables(positions.reshape(T, 1))
    for i in range(depth):
        j = i // N_MIXERS
        kind = i % N_MIXERS
        if kind == 0:
            y = _diff_layer(h, B, S, tabs, norm_gain[i], a_w_in[j], a_q_norm[j], a_k_norm[j],
                            a_lambda_q1[j], a_lambda_k1[j], a_lambda_q2[j], a_lambda_k2[j],
                            a_sub_norm[j], i)
            w_out = a_w_out[j]
        elif kind == 1:
            y = _mla_layer(h, B, S, tabs, norm_gain[i], b_w_in[j], b_cq_norm[j], b_ckv_norm[j],
                           b_w_uq[j], b_w_ukv[j], b_q_nope_norm[j], b_q_rope_norm[j],
                           b_k_nope_norm[j], b_k_rope_norm[j])
            w_out = b_w_out[j]
        else:
            y = _lru_layer(h, B, S, norm_gain[i], c_w_in[j], c_conv_w[j], c_conv_b[j], c_w_a[j],
                           c_b_a[j], c_w_x[j], c_b_x[j], c_lambda[j])
            w_out = c_w_out[j]
        h = _post_mixer(h, y, w_out, p[i].reshape(T, -1), ple_norm[i], ple_w_gate[i], ple_w_proj[i])
    return h.reshape(B, S, D)
```

```python
import functools
import math

import jax
import jax.numpy as jnp
from jax import lax
from jax.experimental import pallas as pl
from jax.experimental.pallas import tpu as pltpu

F32 = jnp.float32
BF16 = jnp.bfloat16

D_MODEL = 2048
CHUNK = 64
ROPE_THETA = 10000.0
EPS = 1e-6
N_MIXERS = 3

DA_HEAD_DIM = 128
DA_V_DIM = 256
DA_HEADS = 8
MLA_NOPE = 128
MLA_ROPE = 64
MLA_V = 128
MLA_HEADS = 16
MLA_Q_RANK = 512
MLA_KV_RANK = 512
LRU_BLOCKS = 8
LRU_BLOCK = 256
CONV_WIDTH = 4
LRU_C = 8.0

LANES = 128
VMEM_LIMIT = 56 * 1024 * 1024
ROW_TILE = 1024
COL_TILE = 1024
PLE_ROW_TILE = 512
PLE_COL_TILE = 512
MLA_HEAD_TILE = 512
PROJ_CHUNK = 512
LOG2E = math.log2(math.e)
ATTN_BLOCK = 512
LRU_TIME_TILE = 256


def _cparams(*sem):
    return pltpu.CompilerParams(dimension_semantics=sem, vmem_limit_bytes=VMEM_LIMIT)


def _rope_table_kernel(pos_ref, inv_da_ref, inv_mla_ref, c_da, s_da, c_m, s_m1, s_m2):
    pos = pos_ref[...].astype(F32)
    lane = lax.broadcasted_iota(jnp.int32, c_da.shape, 1)
    ang = pos * inv_da_ref[...]
    sn = jnp.sin(ang)
    c_da[...] = jnp.cos(ang)
    s_da[...] = jnp.where(lane < 64, -sn, sn)
    ang = pos * inv_mla_ref[...]
    cs = jnp.cos(ang)
    sn = jnp.sin(ang)
    c_m[...] = jnp.where(lane < 64, cs, 0.0)
    s_m1[...] = jnp.where(lane < 32, -sn, 0.0)
    s_m2[...] = jnp.where((lane >= 32) & (lane < 64), sn, 0.0)


def _rope_tables(pos_col):
    T = pos_col.shape[0]
    tr = min(2048, T)
    half = DA_HEAD_DIM // 2
    inv = ROPE_THETA ** (-jnp.arange(half, dtype=F32) / half)
    inv_da = jnp.concatenate([inv, inv])[None, :]
    half = MLA_ROPE // 2
    inv = ROPE_THETA ** (-jnp.arange(half, dtype=F32) / half)
    inv_mla = jnp.concatenate([inv, inv, inv, inv])[None, :]
    tab = jax.ShapeDtypeStruct((T, LANES), F32)
    row = pl.BlockSpec((tr, LANES), lambda i: (i, 0))
    vec = pl.BlockSpec((1, LANES), lambda i: (0, 0))
    return pl.pallas_call(
        _rope_table_kernel,
        out_shape=[tab] * 5,
        grid=(T // tr,),
        in_specs=[pl.BlockSpec((tr, 1), lambda i: (i, 0)), vec, vec],
        out_specs=[row] * 5,
        compiler_params=_cparams("parallel"),
        name="rope_tables",
    )(pos_col, inv_da, inv_mla)


def _proj_kernel(x_ref, g_ref, w_ref, *rest, n_extra, n_out, norm, epilogue):
    extra = rest[:n_extra]
    outs = rest[n_extra:n_extra + n_out]
    scratch = rest[n_extra + n_out:]
    j = pl.program_id(1)
    if norm:
        xn_ref, scratch = scratch[0], scratch[1:]

        @pl.when(j == 0)
        def _():
            x = x_ref[...].astype(F32)
            ms = jnp.mean(x * x, axis=-1, keepdims=True)
            xn_ref[...] = (x * lax.rsqrt(ms + EPS) * g_ref[...]).astype(BF16)

        lhs = xn_ref[...]
    else:
        lhs = x_ref[...]
    def dot(cols):
        return jnp.dot(lhs, w_ref[:, cols], preferred_element_type=F32)

    epilogue(dot, j, extra, outs, *scratch)


def _col_chunks(width):
    return [slice(c, min(c + PROJ_CHUNK, width)) for c in range(0, width, PROJ_CHUNK)]


def _projection(x, w, *, epilogue, out_shapes, out_specs, gain=None, x_col_block=0,
                extras=(), scratch=(), tm=ROW_TILE, tn=COL_TILE, name):
    T = x.shape[0]
    K, N = w.shape
    tm = min(tm, T)
    tn = min(tn, N)
    norm = gain is not None
    if not norm:
        gain = jnp.zeros((1, K), F32)
    in_specs = [
        pl.BlockSpec((tm, K), lambda i, j: (i, x_col_block)),
        pl.BlockSpec((1, K), lambda i, j: (0, 0)),
        pl.BlockSpec((K, tn), lambda i, j: (0, j)),
    ] + [spec for _, spec in extras]
    kern = functools.partial(_proj_kernel, n_extra=len(extras), n_out=len(out_shapes),
                             norm=norm, epilogue=epilogue)
    return pl.pallas_call(
        kern,
        out_shape=out_shapes,
        grid=(T // tm, N // tn),
        in_specs=in_specs,
        out_specs=out_specs,
        scratch_shapes=([pltpu.VMEM((tm, K), BF16)] if norm else []) + list(scratch),
        compiler_params=_cparams("parallel", "arbitrary"),
        name=name,
    )(x, gain.reshape(1, K).astype(F32), w, *[a for a, _ in extras])


def _rms(x, width):
    return x * lax.rsqrt(jnp.sum(x * x, axis=-1, keepdims=True) * (1.0 / width) + EPS)


def _sigmoid(x):
    return 0.5 * jnp.tanh(0.5 * x) + 0.5


def _silu(x):
    return x * jax.nn.sigmoid(x)


def _da_in_epilogue(dot, j, extra, outs):
    g_ref, c_ref, s_ref = extra
    (o_ref,) = outs
    n_qk = 2 * DA_HEADS * 2 * DA_HEAD_DIM // COL_TILE
    n_v = DA_HEADS * DA_V_DIM // COL_TILE
    acc = dot(slice(0, o_ref.shape[1]))

    @pl.when(j < n_qk)
    def _():
        g = g_ref[0]
        c = c_ref[...]
        s = s_ref[...]
        for t in range(acc.shape[1] // DA_HEAD_DIM):
            sl = slice(t * DA_HEAD_DIM, (t + 1) * DA_HEAD_DIM)
            xn = _rms(acc[:, sl], DA_HEAD_DIM) * g
            o_ref[:, sl] = (xn * c + pltpu.roll(xn, 64, 1) * s).astype(BF16)

    @pl.when((j >= n_qk) & (j < n_qk + n_v))
    def _():
        o_ref[...] = acc.astype(BF16)

    @pl.when(j >= n_qk + n_v)
    def _():
        o_ref[...] = _silu(acc).astype(BF16)


def _silu_epilogue(dot, j, extra, outs):
    for cols in _col_chunks(outs[0].shape[1]):
        outs[0][:, cols] = _silu(dot(cols)).astype(BF16)


def _f32_epilogue(dot, j, extra, outs):
    for cols in _col_chunks(outs[0].shape[1]):
        outs[0][:, cols] = dot(cols)


def _mla_rope(zn, c_m, s_m1, s_m2):
    return zn * c_m + pltpu.roll(zn, 96, 1) * s_m1 + pltpu.roll(zn, 32, 1) * s_m2


def _mla_q_epilogue(dot, j, extra, outs):
    gn_ref, gr_ref, c_ref, s1_ref, s2_ref = extra
    (o_ref,) = outs
    for c0 in range(0, o_ref.shape[1], 256):
        acc = dot(slice(c0, c0 + 256))
        o_ref[:, c0:c0 + MLA_NOPE] = (_rms(acc[:, :MLA_NOPE], MLA_NOPE) * gn_ref[...]).astype(BF16)
        zn = _rms(acc[:, MLA_NOPE:], MLA_ROPE) * gr_ref[...]
        o_ref[:, c0 + MLA_NOPE:c0 + 256] = _mla_rope(zn, c_ref[...], s1_ref[...], s2_ref[...]).astype(BF16)


def _mla_kv_epilogue(dot, j, extra, outs, kpe_scr):
    gn_ref, gr_ref, kpe_ref, c_ref, s1_ref, s2_ref = extra
    k_ref, v_ref = outs

    @pl.when(j == 0)
    def _():
        zn = _rms(kpe_ref[...], MLA_ROPE) * gr_ref[...]
        kpe_scr[...] = _mla_rope(zn, c_ref[...], s1_ref[...], s2_ref[...]).astype(BF16)

    for t in range(k_ref.shape[1] // 256):
        c0 = t * 256
        acc = dot(slice(c0, c0 + 256))
        k_ref[:, c0:c0 + MLA_NOPE] = (_rms(acc[:, :MLA_NOPE], MLA_NOPE) * gn_ref[...]).astype(BF16)
        k_ref[:, c0 + MLA_NOPE:c0 + 256] = kpe_scr[...]
        v_ref[:, t * MLA_V:(t + 1) * MLA_V] = acc[:, MLA_NOPE:].astype(BF16)


def _out_proj_kernel(y_ref, w_ref, h_ref, o_ref):
    y = y_ref[...]
    for cols in _col_chunks(o_ref.shape[1]):
        o_ref[:, cols] = h_ref[:, cols] + jnp.dot(y, w_ref[:, cols], preferred_element_type=F32)


def _out_proj(h, y, w_out):
    T, D = h.shape
    K = y.shape[1]
    tm = min(PLE_ROW_TILE, T)
    row = lambda w: pl.BlockSpec((tm, w), lambda i: (i, 0))
    return pl.pallas_call(
        _out_proj_kernel,
        out_shape=jax.ShapeDtypeStruct((T, D), F32),
        grid=(T // tm,),
        in_specs=[row(K), pl.BlockSpec((K, D), lambda i: (0, 0)), row(D)],
        out_specs=row(D),
        compiler_params=_cparams("parallel"),
        name="out_proj",
    )(y, w_out.astype(BF16), h)


def _ple_kernel(x_ref, g_ref, wg_ref, p_ref, wp_ref, o_ref):
    xn = (_rms(x_ref[...], x_ref.shape[1]) * g_ref[...]).astype(BF16)
    pb = p_ref[...].astype(BF16)
    for c0 in range(0, o_ref.shape[1], PLE_COL_TILE):
        cols = slice(c0, c0 + PLE_COL_TILE)
        gate = jnp.dot(xn, wg_ref[:, cols], preferred_element_type=F32)
        proj = jnp.dot(pb, wp_ref[:, cols], preferred_element_type=F32)
        o_ref[:, cols] = x_ref[:, cols] + jax.nn.sigmoid(gate) * proj


def _ple(h, p_i, gain, w_gate, w_proj):
    T, D = h.shape
    P = p_i.shape[1]
    tm = min(PLE_ROW_TILE, T)
    row = lambda w: pl.BlockSpec((tm, w), lambda i: (i, 0))
    const = lambda r, c: pl.BlockSpec((r, c), lambda i: (0, 0))
    return pl.pallas_call(
        _ple_kernel,
        out_shape=jax.ShapeDtypeStruct((T, D), F32),
        grid=(T // tm,),
        in_specs=[row(D), const(1, D), const(D, D), row(P), const(P, D)],
        out_specs=row(D),
        compiler_params=_cparams("parallel"),
        name="ple",
    )(h, gain.reshape(1, D).astype(F32), w_gate.astype(BF16), p_i, w_proj.astype(BF16))


def _chunk_mask(bq, bk):
    r = lax.broadcasted_iota(jnp.int32, (bq, bk), 0) // CHUNK
    c = lax.broadcasted_iota(jnp.int32, (bq, bk), 1) // CHUNK
    return c <= r


def _lane_fold(x, op):
    out = x[:, :LANES]
    for t in range(1, x.shape[1] // LANES):
        out = op(out, x[:, t * LANES:(t + 1) * LANES])
    return out


def _block_loop(n, fn):
    def body(t, carry):
        for u in range(4):
            fn(4 * t + u)
        return carry

    quads = lax.shift_right_logical(n, 2)
    lax.fori_loop(0, quads, body, 0)

    @pl.when((n & 2) == 2)
    def _():
        fn(4 * quads)
        fn(4 * quads + 1)

    @pl.when((n & 1) == 1)
    def _():
        fn(n - 1)


def _scores_phase(qs, k_ref, kcols, s_ref, m_ref, i, bq):
    ncomp = len(qs)

    half = bq // 2
    nt = (((1,), (1,)), ((), ()))

    def block(j):
        rows = pl.ds(pl.multiple_of(j * bq, bq), bq)
        folded = []
        for c in range(ncomp):
            s = lax.dot_general(qs[c], k_ref[0, rows, kcols[c]], nt, preferred_element_type=F32)
            s_ref[c, j] = s
            folded.append(_lane_fold(s, jnp.maximum))
        return folded

    mask = _chunk_mask(half, half)
    base = pl.multiple_of(i * bq, bq)
    for c in range(ncomp):
        s0 = lax.dot_general(qs[c], k_ref[0, pl.ds(base, half), kcols[c]], nt,
                             preferred_element_type=F32)
        s1 = lax.dot_general(qs[c][half:], k_ref[0, pl.ds(base + half, half), kcols[c]], nt,
                             preferred_element_type=F32)
        top = jnp.where(mask, s0[:half], -jnp.inf)
        s1 = jnp.where(mask, s1, -jnp.inf)
        s_ref[c, i, :half, :half] = top
        s_ref[c, i, half:, :half] = s0[half:]
        s_ref[c, i, half:, half:] = s1
        m_ref[c, :half] = _lane_fold(top, jnp.maximum)
        m_ref[c, half:] = jnp.maximum(_lane_fold(s0[half:], jnp.maximum), _lane_fold(s1, jnp.maximum))

    def update(j):
        for c, f in enumerate(block(j)):
            m_ref[c] = jnp.maximum(m_ref[c], f)

    _block_loop(i, update)
    for c in range(ncomp):
        m_ref[c] = jnp.broadcast_to(jnp.max(m_ref[c], axis=-1, keepdims=True), (bq, LANES))


def _values_phase(v_ref, s_ref, m_ref, l_ref, acc_ref, ncomp, i, bq):
    half = bq // 2

    def block(j):
        v = v_ref[0, pl.ds(pl.multiple_of(j * bq, bq), bq), :]
        for c in range(ncomp):
            m = jnp.concatenate([m_ref[c]] * (bq // LANES), axis=1)
            p = jnp.exp2(s_ref[c, j] - m)
            l_ref[c] += _lane_fold(p, jnp.add)
            acc_ref[c] += jnp.dot(p.astype(BF16), v, preferred_element_type=F32)

    base = pl.multiple_of(i * bq, bq)
    v0 = v_ref[0, pl.ds(base, half), :]
    v1 = v_ref[0, pl.ds(base + half, half), :]
    for c in range(ncomp):
        m = jnp.concatenate([m_ref[c]] * (half // LANES), axis=1)
        p0 = jnp.exp2(s_ref[c, i, :, :half] - m)
        p1 = jnp.exp2(s_ref[c, i, half:, half:] - m[half:])
        pv0 = jnp.dot(p0.astype(BF16), v0, preferred_element_type=F32)
        pv1 = jnp.dot(p1.astype(BF16), v1, preferred_element_type=F32)
        lp0 = _lane_fold(p0, jnp.add)
        acc_ref[c, :half] = pv0[:half]
        acc_ref[c, half:] = pv0[half:] + pv1
        l_ref[c, :half] = lp0[:half]
        l_ref[c, half:] = lp0[half:] + _lane_fold(p1, jnp.add)

    _block_loop(i, block)


def _da_attn_kernel(q_ref, k_ref, v_ref, gate_ref, lq1_ref, lk1_ref, lq2_ref, lk2_ref, sg_ref,
                    o_ref, s_ref, m_ref, l_ref, acc_ref, *, bq, lam_init):
    i = pl.program_id(2)
    d = DA_HEAD_DIM
    q = q_ref[0]
    cols = [slice(0, d), slice(d, 2 * d)]
    _scores_phase([q[:, c] for c in cols], k_ref, cols, s_ref, m_ref, i, bq)
    _values_phase(v_ref, s_ref, m_ref, l_ref, acc_ref, 2, i, bq)
    lam = (jnp.exp(jnp.sum(lq1_ref[...] * lk1_ref[...], axis=-1, keepdims=True))
           - jnp.exp(jnp.sum(lq2_ref[...] * lk2_ref[...], axis=-1, keepdims=True)) + lam_init)
    l1 = jnp.sum(l_ref[0], axis=-1, keepdims=True)
    l2 = jnp.sum(l_ref[1], axis=-1, keepdims=True)
    o = acc_ref[0] / l1 - lam * (acc_ref[1] / l2)
    o = _rms(o, DA_V_DIM) * sg_ref[...] * (1.0 - lam_init)
    o_ref[0] = (o * gate_ref[0].astype(F32)).astype(BF16)


def _attn_scratch(ncomp, S, bq, dv):
    return [pltpu.VMEM((ncomp, S // bq, bq, bq), F32), pltpu.VMEM((ncomp, bq, LANES), F32),
            pltpu.VMEM((ncomp, bq, LANES), F32), pltpu.VMEM((ncomp, bq, dv), F32)]


def _da_attention(u3, lq1, lk1, lq2, lk2, sub_gain, lam_init):
    B, S, _ = u3.shape
    bq = min(ATTN_BLOCK, S)
    H = DA_HEADS
    vec = lambda n: pl.BlockSpec((1, n), lambda b, h, i: (0, 0))
    kern = functools.partial(_da_attn_kernel, bq=bq, lam_init=lam_init)
    return pl.pallas_call(
        kern,
        out_shape=jax.ShapeDtypeStruct((B, S, H * DA_V_DIM), BF16),
        grid=(B, H, S // bq),
        in_specs=[
            pl.BlockSpec((1, bq, 256), lambda b, h, i: (b, i, h)),
            pl.BlockSpec((1, S, 256), lambda b, h, i: (b, 0, H + h)),
            pl.BlockSpec((1, S, 256), lambda b, h, i: (b, 0, 2 * H + h)),
            pl.BlockSpec((1, bq, 256), lambda b, h, i: (b, i, 3 * H + h)),
            vec(128), vec(128), vec(128), vec(128), vec(256),
        ],
        out_specs=pl.BlockSpec((1, bq, 256), lambda b, h, i: (b, i, h)),
        scratch_shapes=_attn_scratch(2, S, bq, DA_V_DIM),
        compiler_params=_cparams("parallel", "parallel", "arbitrary"),
        name="da_attention",
    )(u3, u3, u3, u3, lq1.reshape(1, 128), lk1.reshape(1, 128), lq2.reshape(1, 128),
      lk2.reshape(1, 128), sub_gain.reshape(1, 256))


def _mla_attn_kernel(q_ref, k_ref, v_ref, gate_ref, o_ref, s_ref, m_ref, l_ref, acc_ref, *, bq):
    i = pl.program_id(2)
    _scores_phase([q_ref[0]], k_ref, [slice(None)], s_ref, m_ref, i, bq)
    _values_phase(v_ref, s_ref, m_ref, l_ref, acc_ref, 1, i, bq)
    o = acc_ref[0] / jnp.sum(l_ref[0], axis=-1, keepdims=True)
    o_ref[0] = (o * gate_ref[0].astype(F32)).astype(BF16)


def _mla_attention(q3, k3, v3, gate3):
    B, S, _ = q3.shape
    bq = min(ATTN_BLOCK, S)
    H = MLA_HEADS
    kern = functools.partial(_mla_attn_kernel, bq=bq)
    return pl.pallas_call(
        kern,
        out_shape=jax.ShapeDtypeStruct((B, S, H * MLA_V), BF16),
        grid=(B, H, S // bq),
        in_specs=[
            pl.BlockSpec((1, bq, 256), lambda b, h, i: (b, i, h)),
            pl.BlockSpec((1, S, 256), lambda b, h, i: (b, 0, h)),
            pl.BlockSpec((1, S, MLA_V), lambda b, h, i: (b, 0, h)),
            pl.BlockSpec((1, bq, MLA_V), lambda b, h, i: (b, i, h)),
        ],
        out_specs=pl.BlockSpec((1, bq, MLA_V), lambda b, h, i: (b, i, h)),
        scratch_shapes=_attn_scratch(1, S, bq, MLA_V),
        compiler_params=_cparams("parallel", "parallel", "arbitrary"),
        name="mla_attention",
    )(q3, k3, v3, gate3)


def _lru_kernel(xb_ref, gate_ref, cw_ref, cb_ref, wa_ref, ba_ref, wx_ref, bx_ref, lam_ref,
                o_ref, ext_ref, xc_ref, a_ref, b_ref, h_ref, *, ts):
    t = pl.program_id(1)
    W = xb_ref.shape[-1]

    @pl.when(t == 0)
    def _():
        ext_ref[0:8, :] = jnp.zeros((8, W), F32)
        h_ref[...] = jnp.zeros((1, W), F32)

    ext_ref[8:8 + ts, :] = xb_ref[0]
    xc_ref[...] = (cb_ref[...]
                   + cw_ref[3:4, :] * ext_ref[8:8 + ts, :]
                   + cw_ref[2:3, :] * ext_ref[7:7 + ts, :]
                   + cw_ref[1:2, :] * ext_ref[6:6 + ts, :]
                   + cw_ref[0:1, :] * ext_ref[5:5 + ts, :])
    ext_ref[0:8, :] = ext_ref[ts:ts + 8, :]

    nl = -lam_ref[...]
    softplus = jnp.maximum(nl, 0.0) + jnp.log1p(jnp.exp(-jnp.abs(nl)))
    for g in range(LRU_BLOCKS):
        sl = slice(g * LRU_BLOCK, (g + 1) * LRU_BLOCK)
        xc = xc_ref[:, sl]
        xcb = xc.astype(BF16)
        r = _sigmoid(jnp.dot(xcb, wa_ref[g], preferred_element_type=F32) + ba_ref[:, sl])
        ig = _sigmoid(jnp.dot(xcb, wx_ref[g], preferred_element_type=F32) + bx_ref[:, sl])
        log_a = -LRU_C * r * softplus[:, sl]
        th = jnp.tanh(log_a)
        a_ref[:, sl] = jnp.exp(log_a)
        b_ref[:, sl] = jnp.sqrt(-2.0 * th / (1.0 - th)) * (ig * xc)

    def body(r8, h):
        base = pl.multiple_of(r8 * 8, 8)
        for r in range(8):
            h = a_ref[pl.ds(base + r, 1), :] * h + b_ref[pl.ds(base + r, 1), :]
            b_ref[pl.ds(base + r, 1), :] = h
        return h

    h_ref[...] = lax.fori_loop(0, ts // 8, body, h_ref[...])
    o_ref[0] = (b_ref[...] * gate_ref[0].astype(F32)).astype(BF16)


def _lru_mixer(xb3, gate3, conv_w, conv_b, w_a, b_a, w_x, b_x, lam):
    B, S, W = xb3.shape
    ts = min(LRU_TIME_TILE, S)
    vec = pl.BlockSpec((1, W), lambda b, t: (0, 0))
    wblk = pl.BlockSpec((LRU_BLOCKS, LRU_BLOCK, LRU_BLOCK), lambda b, t: (0, 0, 0))
    tile = pl.BlockSpec((1, ts, W), lambda b, t: (b, t, 0))
    kern = functools.partial(_lru_kernel, ts=ts)
    return pl.pallas_call(
        kern,
        out_shape=jax.ShapeDtypeStruct((B, S, W), BF16),
        grid=(B, S // ts),
        in_specs=[tile, tile, pl.BlockSpec((CONV_WIDTH, W), lambda b, t: (0, 0)), vec,
                  wblk, vec, wblk, vec, vec],
        out_specs=tile,
        scratch_shapes=[pltpu.VMEM((ts + 8, W), F32), pltpu.VMEM((ts, W), F32),
                        pltpu.VMEM((ts, W), F32), pltpu.VMEM((ts, W), F32),
                        pltpu.VMEM((1, W), F32)],
        compiler_params=_cparams("parallel", "arbitrary"),
        name="rglru",
    )(xb3, gate3, conv_w, conv_b.reshape(1, W), w_a.astype(BF16), b_a.reshape(1, W),
      w_x.astype(BF16), b_x.reshape(1, W), lam.reshape(1, W))


def _const_spec(shape):
    return pl.BlockSpec(shape, lambda i, j: (0,) * len(shape))


def _diff_layer(h, B, S, tabs, gain, w_in, q_gain, k_gain, lq1, lk1, lq2, lk2, sub_gain, layer_idx):
    T = h.shape[0]
    tm = min(ROW_TILE, T)
    c_da, s_da = tabs[0], tabs[1]
    scale = DA_HEAD_DIM ** -0.5 * LOG2E
    qk_gain = jnp.stack([q_gain * scale, k_gain]).reshape(2, 1, DA_HEAD_DIM).astype(F32)
    n_q = DA_HEADS * 2 * DA_HEAD_DIM // COL_TILE
    tab_spec = pl.BlockSpec((tm, LANES), lambda i, j: (i, 0))
    u = _projection(
        h, w_in.astype(BF16), gain=gain, epilogue=_da_in_epilogue,
        extras=[(qk_gain, pl.BlockSpec((1, 1, DA_HEAD_DIM), lambda i, j: (jnp.minimum(j // n_q, 1), 0, 0))),
                (c_da, tab_spec), (s_da, tab_spec)],
        out_shapes=[jax.ShapeDtypeStruct((T, w_in.shape[1]), BF16)],
        out_specs=[pl.BlockSpec((tm, COL_TILE), lambda i, j: (i, j))],
        name="da_in_proj")[0]
    lam_init = 0.8 - 0.6 * math.exp(-0.3 * layer_idx)
    y = _da_attention(u.reshape(B, S, -1), lq1, lk1, lq2, lk2, sub_gain, lam_init)
    return y.reshape(T, -1)


def _mla_layer(h, B, S, tabs, gain, w_in, cq_gain, ckv_gain, w_uq, w_ukv, qn_gain, qr_gain,
               kn_gain, kr_gain):
    T = h.shape[0]
    tm = min(ROW_TILE, T)
    c_m, s_m1, s_m2 = tabs[2], tabs[3], tabs[4]
    H = MLA_HEADS
    n_lat = MLA_Q_RANK + MLA_KV_RANK + MLA_ROPE
    lat_w = n_lat + (LANES - MLA_ROPE)
    w_lat = jnp.pad(w_in[:, :n_lat], ((0, 0), (0, lat_w - n_lat))).astype(BF16)
    w_gate = w_in[:, n_lat:].astype(BF16)
    lat = _projection(
        h, w_lat, gain=gain, epilogue=_f32_epilogue, tn=lat_w,
        out_shapes=[jax.ShapeDtypeStruct((T, lat_w), F32)],
        out_specs=[pl.BlockSpec((tm, lat_w), lambda i, j: (i, j))],
        name="mla_latent_proj")[0]
    gate = _projection(
        h, w_gate, gain=gain, epilogue=_silu_epilogue,
        out_shapes=[jax.ShapeDtypeStruct((T, w_gate.shape[1]), BF16)],
        out_specs=[pl.BlockSpec((tm, COL_TILE), lambda i, j: (i, j))],
        name="mla_gate_proj")[0]

    scale = (MLA_NOPE + MLA_ROPE) ** -0.5 * LOG2E
    pad_r = lambda g: jnp.pad(g, (0, LANES - MLA_ROPE)).reshape(1, LANES).astype(F32)
    hd = MLA_NOPE + MLA_ROPE
    w_uq_p = jnp.pad(w_uq.reshape(MLA_Q_RANK, H, hd), ((0, 0), (0, 0), (0, 256 - hd)))
    w_uq_p = w_uq_p.reshape(MLA_Q_RANK, H * 256).astype(BF16)
    vec = _const_spec((1, LANES))
    tab_spec = pl.BlockSpec((tm, LANES), lambda i, j: (i, 0))
    tab_extras = [(c_m, tab_spec), (s_m1, tab_spec), (s_m2, tab_spec)]
    q = _projection(
        lat, w_uq_p, gain=cq_gain, x_col_block=0, epilogue=_mla_q_epilogue, tn=MLA_HEAD_TILE,
        extras=[((qn_gain * scale).reshape(1, LANES), vec), (pad_r(qr_gain * scale), vec)] + tab_extras,
        out_shapes=[jax.ShapeDtypeStruct((T, H * 256), BF16)],
        out_specs=[pl.BlockSpec((tm, MLA_HEAD_TILE), lambda i, j: (i, j))],
        name="mla_q_proj")[0]
    kpe_block = (MLA_Q_RANK + MLA_KV_RANK) // LANES
    k, v = _projection(
        lat, w_ukv.astype(BF16), gain=ckv_gain, x_col_block=1, epilogue=_mla_kv_epilogue,
        tn=MLA_HEAD_TILE, scratch=[pltpu.VMEM((tm, LANES), BF16)],
        extras=[(kn_gain.reshape(1, LANES), vec), (pad_r(kr_gain), vec),
                (lat, pl.BlockSpec((tm, LANES), lambda i, j: (i, kpe_block)))] + tab_extras,
        out_shapes=[jax.ShapeDtypeStruct((T, H * 256), BF16), jax.ShapeDtypeStruct((T, H * MLA_V), BF16)],
        out_specs=[pl.BlockSpec((tm, MLA_HEAD_TILE), lambda i, j: (i, j)),
                   pl.BlockSpec((tm, MLA_HEAD_TILE // 2), lambda i, j: (i, j))],
        name="mla_kv_proj")
    y = _mla_attention(q.reshape(B, S, -1), k.reshape(B, S, -1), v.reshape(B, S, -1),
                       gate.reshape(B, S, -1))
    return y.reshape(T, -1)


def _lru_layer(h, B, S, gain, w_in, conv_w, conv_b, w_a, b_a, w_x, b_x, lam):
    T = h.shape[0]
    tm = min(ROW_TILE, T)
    W = D_MODEL
    xb = _projection(
        h, w_in[:, :W].astype(BF16), gain=gain, epilogue=_f32_epilogue,
        out_shapes=[jax.ShapeDtypeStruct((T, W), F32)],
        out_specs=[pl.BlockSpec((tm, COL_TILE), lambda i, j: (i, j))],
        name="lru_x_proj")[0]
    gate = _projection(
        h, w_in[:, W:].astype(BF16), gain=gain, epilogue=_silu_epilogue,
        out_shapes=[jax.ShapeDtypeStruct((T, W), BF16)],
        out_specs=[pl.BlockSpec((tm, COL_TILE), lambda i, j: (i, j))],
        name="lru_gate_proj")[0]
    y = _lru_mixer(xb.reshape(B, S, W), gate.reshape(B, S, W), conv_w, conv_b, w_a, b_a, w_x, b_x, lam)
    return y.reshape(T, W)


def _post_mixer(h, y, w_out, p_i, ple_gain, w_gate, w_proj):
    return _ple(_out_proj(h, y, w_out), p_i, ple_gain, w_gate, w_proj)


def kernel(x, p, positions, norm_gain, a_w_in, a_q_norm, a_k_norm, a_lambda_q1, a_lambda_k1, a_lambda_q2, a_lambda_k2, a_sub_norm, a_w_out, b_w_in, b_cq_norm, b_ckv_norm, b_w_uq, b_w_ukv, b_q_nope_norm, b_q_rope_norm, b_k_nope_norm, b_k_rope_norm, b_w_out, c_w_in, c_conv_w, c_conv_b, c_w_a, c_b_a, c_w_x, c_b_x, c_lambda, c_w_out, ple_norm, ple_w_gate, ple_w_proj):
    B, S, D = x.shape
    T = B * S
    depth = p.shape[0]
    h = x.reshape(T, D)
    tabs = _rope_tables(positions.reshape(T, 1))
    for i in range(depth):
        j = i // N_MIXERS
        kind = i % N_MIXERS
        if kind == 0:
            y = _diff_layer(h, B, S, tabs, norm_gain[i], a_w_in[j], a_q_norm[j], a_k_norm[j],
                            a_lambda_q1[j], a_lambda_k1[j], a_lambda_q2[j], a_lambda_k2[j],
                            a_sub_norm[j], i)
            w_out = a_w_out[j]
        elif kind == 1:
            y = _mla_layer(h, B, S, tabs, norm_gain[i], b_w_in[j], b_cq_norm[j], b_ckv_norm[j],
                           b_w_uq[j], b_w_ukv[j], b_q_nope_norm[j], b_q_rope_norm[j],
                           b_k_nope_norm[j], b_k_rope_norm[j])
            w_out = b_w_out[j]
        else:
            y = _lru_layer(h, B, S, norm_gain[i], c_w_in[j], c_conv_w[j], c_conv_b[j], c_w_a[j],
                           c_b_a[j], c_w_x[j], c_b_x[j], c_lambda[j])
            w_out = c_w_out[j]
        h = _post_mixer(h, y, w_out, p[i].reshape(T, -1), ple_norm[i], ple_w_gate[i], ple_w_proj[i])
    return h.reshape(B, S, D)
```

```python
import functools
import math

import jax
import jax.numpy as jnp
from jax import lax
from jax.experimental import pallas as pl
from jax.experimental.pallas import tpu as pltpu

F32 = jnp.float32
BF16 = jnp.bfloat16

D_MODEL = 2048
CHUNK = 64
ROPE_THETA = 10000.0
EPS = 1e-6
N_MIXERS = 3

DA_HEAD_DIM = 128
DA_V_DIM = 256
DA_HEADS = 8
MLA_NOPE = 128
MLA_ROPE = 64
MLA_V = 128
MLA_HEADS = 16
MLA_Q_RANK = 512
MLA_KV_RANK = 512
LRU_BLOCKS = 8
LRU_BLOCK = 256
CONV_WIDTH = 4
LRU_C = 8.0

LANES = 128
VMEM_LIMIT = 56 * 1024 * 1024
ROW_TILE = 1024
COL_TILE = 1024
PLE_ROW_TILE = 512
PLE_COL_TILE = 512
MLA_HEAD_TILE = 512
PROJ_CHUNK = 512
LOG2E = math.log2(math.e)
ATTN_BLOCK = 512
LRU_TIME_TILE = 256


def _cparams(*sem):
    return pltpu.CompilerParams(dimension_semantics=sem, vmem_limit_bytes=VMEM_LIMIT)


def _rope_table_kernel(pos_ref, inv_da_ref, inv_mla_ref, c_da, s_da, c_m, s_m1, s_m2):
    pos = pos_ref[...].astype(F32)
    lane = lax.broadcasted_iota(jnp.int32, c_da.shape, 1)
    ang = pos * inv_da_ref[...]
    sn = jnp.sin(ang)
    c_da[...] = jnp.cos(ang)
    s_da[...] = jnp.where(lane < 64, -sn, sn)
    ang = pos * inv_mla_ref[...]
    cs = jnp.cos(ang)
    sn = jnp.sin(ang)
    c_m[...] = jnp.where(lane < 64, cs, 0.0)
    s_m1[...] = jnp.where(lane < 32, -sn, 0.0)
    s_m2[...] = jnp.where((lane >= 32) & (lane < 64), sn, 0.0)


def _rope_tables(pos_col):
    T = pos_col.shape[0]
    tr = min(2048, T)
    half = DA_HEAD_DIM // 2
    inv = ROPE_THETA ** (-jnp.arange(half, dtype=F32) / half)
    inv_da = jnp.concatenate([inv, inv])[None, :]
    half = MLA_ROPE // 2
    inv = ROPE_THETA ** (-jnp.arange(half, dtype=F32) / half)
    inv_mla = jnp.concatenate([inv, inv, inv, inv])[None, :]
    tab = jax.ShapeDtypeStruct((T, LANES), F32)
    row = pl.BlockSpec((tr, LANES), lambda i: (i, 0))
    vec = pl.BlockSpec((1, LANES), lambda i: (0, 0))
    return pl.pallas_call(
        _rope_table_kernel,
        out_shape=[tab] * 5,
        grid=(T // tr,),
        in_specs=[pl.BlockSpec((tr, 1), lambda i: (i, 0)), vec, vec],
        out_specs=[row] * 5,
        compiler_params=_cparams("parallel"),
        name="rope_tables",
    )(pos_col, inv_da, inv_mla)


def _proj_kernel(x_ref, g_ref, w_ref, *rest, n_extra, n_out, norm, epilogue):
    extra = rest[:n_extra]
    outs = rest[n_extra:n_extra + n_out]
    scratch = rest[n_extra + n_out:]
    j = pl.program_id(1)
    if norm:
        xn_ref, scratch = scratch[0], scratch[1:]

        @pl.when(j == 0)
        def _():
            x = x_ref[...].astype(F32)
            ms = jnp.mean(x * x, axis=-1, keepdims=True)
            xn_ref[...] = (x * lax.rsqrt(ms + EPS) * g_ref[...]).astype(BF16)

        lhs = xn_ref[...]
    else:
        lhs = x_ref[...]
    def dot(cols):
        return jnp.dot(lhs, w_ref[:, cols], preferred_element_type=F32)

    epilogue(dot, j, extra, outs, *scratch)


def _col_chunks(width):
    return [slice(c, min(c + PROJ_CHUNK, width)) for c in range(0, width, PROJ_CHUNK)]


def _projection(x, w, *, epilogue, out_shapes, out_specs, gain=None, x_col_block=0,
                extras=(), scratch=(), tm=ROW_TILE, tn=COL_TILE, name):
    T = x.shape[0]
    K, N = w.shape
    tm = min(tm, T)
    tn = min(tn, N)
    norm = gain is not None
    if not norm:
        gain = jnp.zeros((1, K), F32)
    in_specs = [
        pl.BlockSpec((tm, K), lambda i, j: (i, x_col_block)),
        pl.BlockSpec((1, K), lambda i, j: (0, 0)),
        pl.BlockSpec((K, tn), lambda i, j: (0, j)),
    ] + [spec for _, spec in extras]
    kern = functools.partial(_proj_kernel, n_extra=len(extras), n_out=len(out_shapes),
                             norm=norm, epilogue=epilogue)
    return pl.pallas_call(
        kern,
        out_shape=out_shapes,
        grid=(T // tm, N // tn),
        in_specs=in_specs,
        out_specs=out_specs,
        scratch_shapes=([pltpu.VMEM((tm, K), BF16)] if norm else []) + list(scratch),
        compiler_params=_cparams("parallel", "arbitrary"),
        name=name,
    )(x, gain.reshape(1, K).astype(F32), w, *[a for a, _ in extras])


def _rms(x, width):
    return x * lax.rsqrt(jnp.sum(x * x, axis=-1, keepdims=True) * (1.0 / width) + EPS)


def _sigmoid(x):
    return 0.5 * jnp.tanh(0.5 * x) + 0.5


def _silu(x):
    return x * jax.nn.sigmoid(x)


def _da_in_epilogue(dot, j, extra, outs):
    g_ref, c_ref, s_ref = extra
    (o_ref,) = outs
    n_qk = 2 * DA_HEADS * 2 * DA_HEAD_DIM // COL_TILE
    n_v = DA_HEADS * DA_V_DIM // COL_TILE
    acc = dot(slice(0, o_ref.shape[1]))

    @pl.when(j < n_qk)
    def _():
        g = g_ref[0]
        c = c_ref[...]
        s = s_ref[...]
        for t in range(acc.shape[1] // DA_HEAD_DIM):
            sl = slice(t * DA_HEAD_DIM, (t + 1) * DA_HEAD_DIM)
            xn = _rms(acc[:, sl], DA_HEAD_DIM) * g
            o_ref[:, sl] = (xn * c + pltpu.roll(xn, 64, 1) * s).astype(BF16)

    @pl.when((j >= n_qk) & (j < n_qk + n_v))
    def _():
        o_ref[...] = acc.astype(BF16)

    @pl.when(j >= n_qk + n_v)
    def _():
        o_ref[...] = _silu(acc).astype(BF16)


def _f32_epilogue(dot, j, extra, outs):
    for cols in _col_chunks(outs[0].shape[1]):
        outs[0][:, cols] = dot(cols)


def _mla_rope(zn, c_m, s_m1, s_m2):
    return zn * c_m + pltpu.roll(zn, 96, 1) * s_m1 + pltpu.roll(zn, 32, 1) * s_m2


def _mla_q_epilogue(dot, j, extra, outs):
    gn_ref, gr_ref, c_ref, s1_ref, s2_ref = extra
    (o_ref,) = outs
    for c0 in range(0, o_ref.shape[1], 256):
        acc = dot(slice(c0, c0 + 256))
        o_ref[:, c0:c0 + MLA_NOPE] = (_rms(acc[:, :MLA_NOPE], MLA_NOPE) * gn_ref[...]).astype(BF16)
        zn = _rms(acc[:, MLA_NOPE:], MLA_ROPE) * gr_ref[...]
        o_ref[:, c0 + MLA_NOPE:c0 + 256] = _mla_rope(zn, c_ref[...], s1_ref[...], s2_ref[...]).astype(BF16)


def _mla_kv_epilogue(dot, j, extra, outs, kpe_scr):
    gn_ref, gr_ref, kpe_ref, c_ref, s1_ref, s2_ref = extra
    k_ref, v_ref = outs

    @pl.when(j == 0)
    def _():
        zn = _rms(kpe_ref[...], MLA_ROPE) * gr_ref[...]
        kpe_scr[...] = _mla_rope(zn, c_ref[...], s1_ref[...], s2_ref[...]).astype(BF16)

    for t in range(k_ref.shape[1] // 256):
        c0 = t * 256
        acc = dot(slice(c0, c0 + 256))
        k_ref[:, c0:c0 + MLA_NOPE] = (_rms(acc[:, :MLA_NOPE], MLA_NOPE) * gn_ref[...]).astype(BF16)
        k_ref[:, c0 + MLA_NOPE:c0 + 256] = kpe_scr[...]
        v_ref[:, t * MLA_V:(t + 1) * MLA_V] = acc[:, MLA_NOPE:].astype(BF16)


def _norm_proj_kernel(x_ref, g_ref, w_ref, o_ref, *, silu):
    xn = (_rms(x_ref[...], x_ref.shape[1]) * g_ref[...]).astype(BF16)
    for cols in _col_chunks(o_ref.shape[1]):
        acc = jnp.dot(xn, w_ref[:, cols], preferred_element_type=F32)
        o_ref[:, cols] = (_silu(acc) if silu else acc).astype(o_ref.dtype)


def _norm_proj(h, gain, w, *, silu, out_dtype, name):
    T, K = h.shape
    N = w.shape[1]
    tm = min(PLE_ROW_TILE, T)
    return pl.pallas_call(
        functools.partial(_norm_proj_kernel, silu=silu),
        out_shape=jax.ShapeDtypeStruct((T, N), out_dtype),
        grid=(T // tm,),
        in_specs=[pl.BlockSpec((tm, K), lambda i: (i, 0)), pl.BlockSpec((1, K), lambda i: (0, 0)),
                  pl.BlockSpec((K, N), lambda i: (0, 0))],
        out_specs=pl.BlockSpec((tm, N), lambda i: (i, 0)),
        compiler_params=_cparams("parallel"),
        name=name,
    )(h, gain.reshape(1, K).astype(F32), w)


def _out_proj_kernel(y_ref, w_ref, h_ref, o_ref):
    y = y_ref[...]
    for cols in _col_chunks(o_ref.shape[1]):
        o_ref[:, cols] = h_ref[:, cols] + jnp.dot(y, w_ref[:, cols], preferred_element_type=F32)


def _out_proj(h, y, w_out):
    T, D = h.shape
    K = y.shape[1]
    tm = min(PLE_ROW_TILE, T)
    row = lambda w: pl.BlockSpec((tm, w), lambda i: (i, 0))
    return pl.pallas_call(
        _out_proj_kernel,
        out_shape=jax.ShapeDtypeStruct((T, D), F32),
        grid=(T // tm,),
        in_specs=[row(K), pl.BlockSpec((K, D), lambda i: (0, 0)), row(D)],
        out_specs=row(D),
        compiler_params=_cparams("parallel"),
        name="out_proj",
    )(y, w_out.astype(BF16), h)


def _ple_kernel(x_ref, g_ref, wg_ref, p_ref, wp_ref, o_ref):
    xn = (_rms(x_ref[...], x_ref.shape[1]) * g_ref[...]).astype(BF16)
    pb = p_ref[...].astype(BF16)
    for c0 in range(0, o_ref.shape[1], PLE_COL_TILE):
        cols = slice(c0, c0 + PLE_COL_TILE)
        gate = jnp.dot(xn, wg_ref[:, cols], preferred_element_type=F32)
        proj = jnp.dot(pb, wp_ref[:, cols], preferred_element_type=F32)
        o_ref[:, cols] = x_ref[:, cols] + jax.nn.sigmoid(gate) * proj


def _ple(h, p_i, gain, w_gate, w_proj):
    T, D = h.shape
    P = p_i.shape[1]
    tm = min(PLE_ROW_TILE, T)
    row = lambda w: pl.BlockSpec((tm, w), lambda i: (i, 0))
    const = lambda r, c: pl.BlockSpec((r, c), lambda i: (0, 0))
    return pl.pallas_call(
        _ple_kernel,
        out_shape=jax.ShapeDtypeStruct((T, D), F32),
        grid=(T // tm,),
        in_specs=[row(D), const(1, D), const(D, D), row(P), const(P, D)],
        out_specs=row(D),
        compiler_params=_cparams("parallel"),
        name="ple",
    )(h, gain.reshape(1, D).astype(F32), w_gate.astype(BF16), p_i, w_proj.astype(BF16))


def _chunk_mask(bq, bk):
    r = lax.broadcasted_iota(jnp.int32, (bq, bk), 0) // CHUNK
    c = lax.broadcasted_iota(jnp.int32, (bq, bk), 1) // CHUNK
    return c <= r


def _lane_fold(x, op):
    out = x[:, :LANES]
    for t in range(1, x.shape[1] // LANES):
        out = op(out, x[:, t * LANES:(t + 1) * LANES])
    return out


def _block_loop(n, fn):
    def body(t, carry):
        for u in range(4):
            fn(4 * t + u)
        return carry

    quads = lax.shift_right_logical(n, 2)
    lax.fori_loop(0, quads, body, 0)

    @pl.when((n & 2) == 2)
    def _():
        fn(4 * quads)
        fn(4 * quads + 1)

    @pl.when((n & 1) == 1)
    def _():
        fn(n - 1)


def _scores_phase(qs, k_ref, kcols, s_ref, m_ref, i, bq):
    ncomp = len(qs)

    half = bq // 2
    nt = (((1,), (1,)), ((), ()))

    def block(j):
        rows = pl.ds(pl.multiple_of(j * bq, bq), bq)
        folded = []
        for c in range(ncomp):
            s = lax.dot_general(qs[c], k_ref[0, rows, kcols[c]], nt, preferred_element_type=F32)
            s_ref[c, j] = s
            folded.append(_lane_fold(s, jnp.maximum))
        return folded

    mask = _chunk_mask(half, half)
    base = pl.multiple_of(i * bq, bq)
    for c in range(ncomp):
        s0 = lax.dot_general(qs[c], k_ref[0, pl.ds(base, half), kcols[c]], nt,
                             preferred_element_type=F32)
        s1 = lax.dot_general(qs[c][half:], k_ref[0, pl.ds(base + half, half), kcols[c]], nt,
                             preferred_element_type=F32)
        top = jnp.where(mask, s0[:half], -jnp.inf)
        s1 = jnp.where(mask, s1, -jnp.inf)
        s_ref[c, i, :half, :half] = top
        s_ref[c, i, half:, :half] = s0[half:]
        s_ref[c, i, half:, half:] = s1
        m_ref[c, :half] = _lane_fold(top, jnp.maximum)
        m_ref[c, half:] = jnp.maximum(_lane_fold(s0[half:], jnp.maximum), _lane_fold(s1, jnp.maximum))

    def update(j):
        for c, f in enumerate(block(j)):
            m_ref[c] = jnp.maximum(m_ref[c], f)

    _block_loop(i, update)
    for c in range(ncomp):
        m_ref[c] = jnp.broadcast_to(jnp.max(m_ref[c], axis=-1, keepdims=True), (bq, LANES))


def _values_phase(v_ref, s_ref, m_ref, l_ref, acc_ref, ncomp, i, bq):
    half = bq // 2

    def block(j):
        v = v_ref[0, pl.ds(pl.multiple_of(j * bq, bq), bq), :]
        for c in range(ncomp):
            m = jnp.concatenate([m_ref[c]] * (bq // LANES), axis=1)
            p = jnp.exp2(s_ref[c, j] - m)
            l_ref[c] += _lane_fold(p, jnp.add)
            acc_ref[c] += jnp.dot(p.astype(BF16), v, preferred_element_type=F32)

    base = pl.multiple_of(i * bq, bq)
    v0 = v_ref[0, pl.ds(base, half), :]
    v1 = v_ref[0, pl.ds(base + half, half), :]
    for c in range(ncomp):
        m = jnp.concatenate([m_ref[c]] * (half // LANES), axis=1)
        p0 = jnp.exp2(s_ref[c, i, :, :half] - m)
        p1 = jnp.exp2(s_ref[c, i, half:, half:] - m[half:])
        pv0 = jnp.dot(p0.astype(BF16), v0, preferred_element_type=F32)
        pv1 = jnp.dot(p1.astype(BF16), v1, preferred_element_type=F32)
        lp0 = _lane_fold(p0, jnp.add)
        acc_ref[c, :half] = pv0[:half]
        acc_ref[c, half:] = pv0[half:] + pv1
        l_ref[c, :half] = lp0[:half]
        l_ref[c, half:] = lp0[half:] + _lane_fold(p1, jnp.add)

    _block_loop(i, block)


def _da_attn_kernel(q_ref, k_ref, v_ref, gate_ref, lq1_ref, lk1_ref, lq2_ref, lk2_ref, sg_ref,
                    o_ref, s_ref, m_ref, l_ref, acc_ref, *, bq, lam_init):
    i = pl.program_id(2)
    d = DA_HEAD_DIM
    q = q_ref[0]
    cols = [slice(0, d), slice(d, 2 * d)]
    _scores_phase([q[:, c] for c in cols], k_ref, cols, s_ref, m_ref, i, bq)
    _values_phase(v_ref, s_ref, m_ref, l_ref, acc_ref, 2, i, bq)
    lam = (jnp.exp(jnp.sum(lq1_ref[...] * lk1_ref[...], axis=-1, keepdims=True))
           - jnp.exp(jnp.sum(lq2_ref[...] * lk2_ref[...], axis=-1, keepdims=True)) + lam_init)
    l1 = jnp.sum(l_ref[0], axis=-1, keepdims=True)
    l2 = jnp.sum(l_ref[1], axis=-1, keepdims=True)
    o = acc_ref[0] / l1 - lam * (acc_ref[1] / l2)
    o = _rms(o, DA_V_DIM) * sg_ref[...] * (1.0 - lam_init)
    o_ref[0] = (o * gate_ref[0].astype(F32)).astype(BF16)


def _attn_scratch(ncomp, S, bq, dv):
    return [pltpu.VMEM((ncomp, S // bq, bq, bq), F32), pltpu.VMEM((ncomp, bq, LANES), F32),
            pltpu.VMEM((ncomp, bq, LANES), F32), pltpu.VMEM((ncomp, bq, dv), F32)]


def _da_attention(u3, lq1, lk1, lq2, lk2, sub_gain, lam_init):
    B, S, _ = u3.shape
    bq = min(ATTN_BLOCK, S)
    H = DA_HEADS
    vec = lambda n: pl.BlockSpec((1, n), lambda b, h, i: (0, 0))
    kern = functools.partial(_da_attn_kernel, bq=bq, lam_init=lam_init)
    return pl.pallas_call(
        kern,
        out_shape=jax.ShapeDtypeStruct((B, S, H * DA_V_DIM), BF16),
        grid=(B, H, S // bq),
        in_specs=[
            pl.BlockSpec((1, bq, 256), lambda b, h, i: (b, i, h)),
            pl.BlockSpec((1, S, 256), lambda b, h, i: (b, 0, H + h)),
            pl.BlockSpec((1, S, 256), lambda b, h, i: (b, 0, 2 * H + h)),
            pl.BlockSpec((1, bq, 256), lambda b, h, i: (b, i, 3 * H + h)),
            vec(128), vec(128), vec(128), vec(128), vec(256),
        ],
        out_specs=pl.BlockSpec((1, bq, 256), lambda b, h, i: (b, i, h)),
        scratch_shapes=_attn_scratch(2, S, bq, DA_V_DIM),
        compiler_params=_cparams("parallel", "parallel", "arbitrary"),
        name="da_attention",
    )(u3, u3, u3, u3, lq1.reshape(1, 128), lk1.reshape(1, 128), lq2.reshape(1, 128),
      lk2.reshape(1, 128), sub_gain.reshape(1, 256))


def _mla_attn_kernel(q_ref, k_ref, v_ref, gate_ref, o_ref, s_ref, m_ref, l_ref, acc_ref, *, bq):
    i = pl.program_id(2)
    _scores_phase([q_ref[0]], k_ref, [slice(None)], s_ref, m_ref, i, bq)
    _values_phase(v_ref, s_ref, m_ref, l_ref, acc_ref, 1, i, bq)
    o = acc_ref[0] / jnp.sum(l_ref[0], axis=-1, keepdims=True)
    o_ref[0] = (o * gate_ref[0].astype(F32)).astype(BF16)


def _mla_attention(q3, k3, v3, gate3):
    B, S, _ = q3.shape
    bq = min(ATTN_BLOCK, S)
    H = MLA_HEADS
    kern = functools.partial(_mla_attn_kernel, bq=bq)
    return pl.pallas_call(
        kern,
        out_shape=jax.ShapeDtypeStruct((B, S, H * MLA_V), BF16),
        grid=(B, H, S // bq),
        in_specs=[
            pl.BlockSpec((1, bq, 256), lambda b, h, i: (b, i, h)),
            pl.BlockSpec((1, S, 256), lambda b, h, i: (b, 0, h)),
            pl.BlockSpec((1, S, MLA_V), lambda b, h, i: (b, 0, h)),
            pl.BlockSpec((1, bq, MLA_V), lambda b, h, i: (b, i, h)),
        ],
        out_specs=pl.BlockSpec((1, bq, MLA_V), lambda b, h, i: (b, i, h)),
        scratch_shapes=_attn_scratch(1, S, bq, MLA_V),
        compiler_params=_cparams("parallel", "parallel", "arbitrary"),
        name="mla_attention",
    )(q3, k3, v3, gate3)


def _lru_kernel(xb_ref, gate_ref, cw_ref, cb_ref, wa_ref, ba_ref, wx_ref, bx_ref, lam_ref,
                o_ref, ext_ref, xc_ref, a_ref, b_ref, h_ref, *, ts):
    t = pl.program_id(1)
    W = xb_ref.shape[-1]

    @pl.when(t == 0)
    def _():
        ext_ref[0:8, :] = jnp.zeros((8, W), F32)
        h_ref[...] = jnp.zeros((1, W), F32)

    ext_ref[8:8 + ts, :] = xb_ref[0]
    xc_ref[...] = (cb_ref[...]
                   + cw_ref[3:4, :] * ext_ref[8:8 + ts, :]
                   + cw_ref[2:3, :] * ext_ref[7:7 + ts, :]
                   + cw_ref[1:2, :] * ext_ref[6:6 + ts, :]
                   + cw_ref[0:1, :] * ext_ref[5:5 + ts, :])
    ext_ref[0:8, :] = ext_ref[ts:ts + 8, :]

    nl = -lam_ref[...]
    softplus = jnp.maximum(nl, 0.0) + jnp.log1p(jnp.exp(-jnp.abs(nl)))
    for g in range(LRU_BLOCKS):
        sl = slice(g * LRU_BLOCK, (g + 1) * LRU_BLOCK)
        xc = xc_ref[:, sl]
        xcb = xc.astype(BF16)
        r = _sigmoid(jnp.dot(xcb, wa_ref[g], preferred_element_type=F32) + ba_ref[:, sl])
        ig = _sigmoid(jnp.dot(xcb, wx_ref[g], preferred_element_type=F32) + bx_ref[:, sl])
        log_a = -LRU_C * r * softplus[:, sl]
        th = jnp.tanh(log_a)
        a_ref[:, sl] = jnp.exp(log_a)
        b_ref[:, sl] = jnp.sqrt(-2.0 * th / (1.0 - th)) * (ig * xc)

    def body(r8, h):
        base = pl.multiple_of(r8 * 8, 8)
        for r in range(8):
            h = a_ref[pl.ds(base + r, 1), :] * h + b_ref[pl.ds(base + r, 1), :]
            b_ref[pl.ds(base + r, 1), :] = h
        return h

    h_ref[...] = lax.fori_loop(0, ts // 8, body, h_ref[...])
    o_ref[0] = (b_ref[...] * gate_ref[0].astype(F32)).astype(BF16)


def _lru_mixer(xb3, gate3, conv_w, conv_b, w_a, b_a, w_x, b_x, lam):
    B, S, W = xb3.shape
    ts = min(LRU_TIME_TILE, S)
    vec = pl.BlockSpec((1, W), lambda b, t: (0, 0))
    wblk = pl.BlockSpec((LRU_BLOCKS, LRU_BLOCK, LRU_BLOCK), lambda b, t: (0, 0, 0))
    tile = pl.BlockSpec((1, ts, W), lambda b, t: (b, t, 0))
    kern = functools.partial(_lru_kernel, ts=ts)
    return pl.pallas_call(
        kern,
        out_shape=jax.ShapeDtypeStruct((B, S, W), BF16),
        grid=(B, S // ts),
        in_specs=[tile, tile, pl.BlockSpec((CONV_WIDTH, W), lambda b, t: (0, 0)), vec,
                  wblk, vec, wblk, vec, vec],
        out_specs=tile,
        scratch_shapes=[pltpu.VMEM((ts + 8, W), F32), pltpu.VMEM((ts, W), F32),
                        pltpu.VMEM((ts, W), F32), pltpu.VMEM((ts, W), F32),
                        pltpu.VMEM((1, W), F32)],
        compiler_params=_cparams("parallel", "arbitrary"),
        name="rglru",
    )(xb3, gate3, conv_w, conv_b.reshape(1, W), w_a.astype(BF16), b_a.reshape(1, W),
      w_x.astype(BF16), b_x.reshape(1, W), lam.reshape(1, W))


def _const_spec(shape):
    return pl.BlockSpec(shape, lambda i, j: (0,) * len(shape))


def _diff_layer(h, B, S, tabs, gain, w_in, q_gain, k_gain, lq1, lk1, lq2, lk2, sub_gain, layer_idx):
    T = h.shape[0]
    tm = min(ROW_TILE, T)
    c_da, s_da = tabs[0], tabs[1]
    scale = DA_HEAD_DIM ** -0.5 * LOG2E
    qk_gain = jnp.stack([q_gain * scale, k_gain]).reshape(2, 1, DA_HEAD_DIM).astype(F32)
    n_q = DA_HEADS * 2 * DA_HEAD_DIM // COL_TILE
    tab_spec = pl.BlockSpec((tm, LANES), lambda i, j: (i, 0))
    u = _projection(
        h, w_in.astype(BF16), gain=gain, epilogue=_da_in_epilogue,
        extras=[(qk_gain, pl.BlockSpec((1, 1, DA_HEAD_DIM), lambda i, j: (jnp.minimum(j // n_q, 1), 0, 0))),
                (c_da, tab_spec), (s_da, tab_spec)],
        out_shapes=[jax.ShapeDtypeStruct((T, w_in.shape[1]), BF16)],
        out_specs=[pl.BlockSpec((tm, COL_TILE), lambda i, j: (i, j))],
        name="da_in_proj")[0]
    lam_init = 0.8 - 0.6 * math.exp(-0.3 * layer_idx)
    y = _da_attention(u.reshape(B, S, -1), lq1, lk1, lq2, lk2, sub_gain, lam_init)
    return y.reshape(T, -1)


def _mla_layer(h, B, S, tabs, gain, w_in, cq_gain, ckv_gain, w_uq, w_ukv, qn_gain, qr_gain,
               kn_gain, kr_gain):
    T = h.shape[0]
    tm = min(ROW_TILE, T)
    c_m, s_m1, s_m2 = tabs[2], tabs[3], tabs[4]
    H = MLA_HEADS
    n_lat = MLA_Q_RANK + MLA_KV_RANK + MLA_ROPE
    lat_w = n_lat + (LANES - MLA_ROPE)
    w_lat = jnp.pad(w_in[:, :n_lat], ((0, 0), (0, lat_w - n_lat))).astype(BF16)
    w_gate = w_in[:, n_lat:].astype(BF16)
    lat = _projection(
        h, w_lat, gain=gain, epilogue=_f32_epilogue, tn=lat_w,
        out_shapes=[jax.ShapeDtypeStruct((T, lat_w), F32)],
        out_specs=[pl.BlockSpec((tm, lat_w), lambda i, j: (i, j))],
        name="mla_latent_proj")[0]
    gate = _norm_proj(h, gain, w_gate, silu=True, out_dtype=BF16, name="mla_gate_proj")

    scale = (MLA_NOPE + MLA_ROPE) ** -0.5 * LOG2E
    pad_r = lambda g: jnp.pad(g, (0, LANES - MLA_ROPE)).reshape(1, LANES).astype(F32)
    hd = MLA_NOPE + MLA_ROPE
    w_uq_p = jnp.pad(w_uq.reshape(MLA_Q_RANK, H, hd), ((0, 0), (0, 0), (0, 256 - hd)))
    w_uq_p = w_uq_p.reshape(MLA_Q_RANK, H * 256).astype(BF16)
    vec = _const_spec((1, LANES))
    tab_spec = pl.BlockSpec((tm, LANES), lambda i, j: (i, 0))
    tab_extras = [(c_m, tab_spec), (s_m1, tab_spec), (s_m2, tab_spec)]
    q = _projection(
        lat, w_uq_p, gain=cq_gain, x_col_block=0, epilogue=_mla_q_epilogue, tn=MLA_HEAD_TILE,
        extras=[((qn_gain * scale).reshape(1, LANES), vec), (pad_r(qr_gain * scale), vec)] + tab_extras,
        out_shapes=[jax.ShapeDtypeStruct((T, H * 256), BF16)],
        out_specs=[pl.BlockSpec((tm, MLA_HEAD_TILE), lambda i, j: (i, j))],
        name="mla_q_proj")[0]
    kpe_block = (MLA_Q_RANK + MLA_KV_RANK) // LANES
    k, v = _projection(
        lat, w_ukv.astype(BF16), gain=ckv_gain, x_col_block=1, epilogue=_mla_kv_epilogue,
        tn=MLA_HEAD_TILE, scratch=[pltpu.VMEM((tm, LANES), BF16)],
        extras=[(kn_gain.reshape(1, LANES), vec), (pad_r(kr_gain), vec),
                (lat, pl.BlockSpec((tm, LANES), lambda i, j: (i, kpe_block)))] + tab_extras,
        out_shapes=[jax.ShapeDtypeStruct((T, H * 256), BF16), jax.ShapeDtypeStruct((T, H * MLA_V), BF16)],
        out_specs=[pl.BlockSpec((tm, MLA_HEAD_TILE), lambda i, j: (i, j)),
                   pl.BlockSpec((tm, MLA_HEAD_TILE // 2), lambda i, j: (i, j))],
        name="mla_kv_proj")
    y = _mla_attention(q.reshape(B, S, -1), k.reshape(B, S, -1), v.reshape(B, S, -1),
                       gate.reshape(B, S, -1))
    return y.reshape(T, -1)


def _lru_layer(h, B, S, gain, w_in, conv_w, conv_b, w_a, b_a, w_x, b_x, lam):
    T = h.shape[0]
    tm = min(ROW_TILE, T)
    W = D_MODEL
    xb = _norm_proj(h, gain, w_in[:, :W].astype(BF16), silu=False, out_dtype=F32, name="lru_x_proj")
    gate = _norm_proj(h, gain, w_in[:, W:].astype(BF16), silu=True, out_dtype=BF16, name="lru_gate_proj")
    y = _lru_mixer(xb.reshape(B, S, W), gate.reshape(B, S, W), conv_w, conv_b, w_a, b_a, w_x, b_x, lam)
    return y.reshape(T, W)


def _post_mixer(h, y, w_out, p_i, ple_gain, w_gate, w_proj):
    return _ple(_out_proj(h, y, w_out), p_i, ple_gain, w_gate, w_proj)


def kernel(x, p, positions, norm_gain, a_w_in, a_q_norm, a_k_norm, a_lambda_q1, a_lambda_k1, a_lambda_q2, a_lambda_k2, a_sub_norm, a_w_out, b_w_in, b_cq_norm, b_ckv_norm, b_w_uq, b_w_ukv, b_q_nope_norm, b_q_rope_norm, b_k_nope_norm, b_k_rope_norm, b_w_out, c_w_in, c_conv_w, c_conv_b, c_w_a, c_b_a, c_w_x, c_b_x, c_lambda, c_w_out, ple_norm, ple_w_gate, ple_w_proj):
    B, S, D = x.shape
    T = B * S
    depth = p.shape[0]
    h = x.reshape(T, D)
    tabs = _rope_tables(positions.reshape(T, 1))
    for i in range(depth):
        j = i // N_MIXERS
        kind = i % N_MIXERS
        if kind == 0:
            y = _diff_layer(h, B, S, tabs, norm_gain[i], a_w_in[j], a_q_norm[j], a_k_norm[j],
                            a_lambda_q1[j], a_lambda_k1[j], a_lambda_q2[j], a_lambda_k2[j],
                            a_sub_norm[j], i)
            w_out = a_w_out[j]
        elif kind == 1:
            y = _mla_layer(h, B, S, tabs, norm_gain[i], b_w_in[j], b_cq_norm[j], b_ckv_norm[j],
                           b_w_uq[j], b_w_ukv[j], b_q_nope_norm[j], b_q_rope_norm[j],
                           b_k_nope_norm[j], b_k_rope_norm[j])
            w_out = b_w_out[j]
        else:
            y = _lru_layer(h, B, S, norm_gain[i], c_w_in[j], c_conv_w[j], c_conv_b[j], c_w_a[j],
                           c_b_a[j], c_w_x[j], c_b_x[j], c_lambda[j])
            w_out = c_w_out[j]
        h = _post_mixer(h, y, w_out, p[i].reshape(T, -1), ple_norm[i], ple_w_gate[i], ple_w_proj[i])
    return h.reshape(B, S, D)
```
